```python
import math
import jax, jax.numpy as jnp
from jax import lax
import numpy as np

D_MODEL = 2048
BATCH = 4
SEQ = 2048
DEPTH = 2
DEC_BATCH = 128
DEC_SEQ = 4
PAST_LEN = 2048
PAGE_SIZE = 128

N_HEADS = 8
HEAD_DIM = 128
ATTN_W = N_HEADS * HEAD_DIM
MOBA_BLOCK = 256
MOBA_TOPK = 3
Q_CHUNK = 16
POOL_WINDOWS = (2, 4, 8, 16)
POOL_GROUPS = 4
POOL_GW = 256
POOL_W = POOL_GROUPS * POOL_GW
POOL_HIST = max(POOL_WINDOWS) - 1
N_EXPERT_GROUPS = 4
EXPERTS_PER_GROUP = 8
N_EXPERTS = N_EXPERT_GROUPS * EXPERTS_PER_GROUP
TOPK_IN_GROUP = 2
D_EXPERT = 512
MOE_BLOCK = 128
PLE_DIM = 256
RMS_EPS = 1e-6
PROJ_W = 3 * ATTN_W + POOL_W + 2 * D_MODEL
NEG = -1e30
F32 = jnp.float32

kernel_name = "hybrid_moba_pool_hmoe_decode_step"


def rmsnorm(x, g):
    xf = x.astype(F32)
    y = xf * lax.rsqrt(jnp.mean(xf * xf, axis=-1, keepdims=True) + RMS_EPS)
    return y.astype(x.dtype) * g


def split_proj(proj):
    lead = proj.shape[:-1]
    o1, o2, o3 = ATTN_W, 2 * ATTN_W, 3 * ATTN_W
    o4 = o3 + POOL_W
    o5 = o4 + D_MODEL
    heads = lambda t: t.reshape(*lead, N_HEADS, HEAD_DIM)
    return (heads(proj[..., :o1]), heads(proj[..., o1:o2]), heads(proj[..., o2:o3]),
            proj[..., o3:o4], proj[..., o4:o5], proj[..., o5:])


def moba_prompt(q, k, v):
    B, S = q.shape[0], q.shape[1]
    nb = -(-S // MOBA_BLOCK)
    sp = nb * MOBA_BLOCK
    pad = ((0, 0), (0, sp - S), (0, 0), (0, 0))
    qh = jnp.pad(q, pad).transpose(0, 2, 1, 3)
    kb = jnp.pad(k, pad).transpose(0, 2, 1, 3).reshape(B, N_HEADS, nb, MOBA_BLOCK, HEAD_DIM)
    vb = jnp.pad(v, pad).transpose(0, 2, 1, 3).reshape(B, N_HEADS, nb, MOBA_BLOCK, HEAD_DIM)
    scale = HEAD_DIM ** -0.5
    k_sel = min(MOBA_TOPK, nb - 1)
    n_past = np.arange(sp) // MOBA_BLOCK
    if k_sel > 0:
        kmean = jnp.mean(kb.astype(F32), axis=3)
        gate = jnp.einsum('bhtd,bhnd->bhtn', qh.astype(F32), kmean)
        gate = jnp.where(jnp.asarray(np.arange(nb)[None, :] < n_past[:, None]), gate, -jnp.inf)
        _, sel = lax.top_k(gate, k_sel)
        sel_valid = jnp.asarray(np.arange(k_sel)[None, :] < n_past[:, None])
    bi = jnp.arange(B)[:, None, None, None]
    hi = jnp.arange(N_HEADS)[None, :, None, None]

    def chunk(c):
        t0 = c * Q_CHUNK
        qc = lax.dynamic_slice_in_dim(qh, t0, Q_CHUNK, axis=2).astype(F32) * scale
        qpos = t0 + jnp.arange(Q_CHUNK)
        blk = t0 // MOBA_BLOCK
        k_own = lax.dynamic_index_in_dim(kb, blk, axis=2, keepdims=False)
        v_own = lax.dynamic_index_in_dim(vb, blk, axis=2, keepdims=False)
        kpos = blk * MOBA_BLOCK + jnp.arange(MOBA_BLOCK)
        l_own = jnp.einsum('bhqd,bhkd->bhqk', qc, k_own.astype(F32))
        l_own = jnp.where(kpos[None, :] <= qpos[:, None], l_own, NEG)
        if k_sel > 0:
            sc = lax.dynamic_slice_in_dim(sel, t0, Q_CHUNK, axis=2)
            vc = lax.dynamic_slice_in_dim(sel_valid, t0, Q_CHUNK, axis=0)
            k_g = kb[bi, hi, sc]
            v_g = vb[bi, hi, sc]
            l_sel = jnp.einsum('bhqd,bhqrkd->bhqrk', qc, k_g.astype(F32))
            l_sel = jnp.where(vc[None, None, :, :, None], l_sel, NEG)
            n_sel = k_sel * MOBA_BLOCK
            l_sel = l_sel.reshape(B, N_HEADS, Q_CHUNK, n_sel)
            p = jax.nn.softmax(jnp.concatenate([l_sel, l_own], axis=-1), axis=-1)
            o = (jnp.einsum('bhqrk,bhqrkd->bhqd',
                            p[..., :n_sel].reshape(B, N_HEADS, Q_CHUNK, k_sel, MOBA_BLOCK), v_g.astype(F32))
                 + jnp.einsum('bhqk,bhkd->bhqd', p[..., n_sel:], v_own.astype(F32)))
        else:
            p = jax.nn.softmax(l_own, axis=-1)
            o = jnp.einsum('bhqk,bhkd->bhqd', p, v_own.astype(F32))
        return o.astype(q.dtype)

    out = lax.map(chunk, jnp.arange(sp // Q_CHUNK))
    out = out.transpose(1, 0, 3, 2, 4).reshape(B, sp, N_HEADS, HEAD_DIM)
    return out[:, :S]


def moba_sample(q, k_new, v_new, k_pool, v_pool, page_table):
    T = q.shape[1]
    ppb = MOBA_BLOCK // PAGE_SIZE
    nbp = PAST_LEN // MOBA_BLOCK
    own_past = PAST_LEN % MOBA_BLOCK
    own_first_page = (PAST_LEN - own_past) // PAGE_SIZE
    n_own_pages = own_past // PAGE_SIZE
    k_sel = min(MOBA_TOPK, nbp)
    n_past = np.minimum((PAST_LEN + np.arange(T)) // MOBA_BLOCK, nbp)
    scale = HEAD_DIM ** -0.5
    causal_new = jnp.asarray(np.tril(np.ones((T, T), dtype=bool))[:, None, :])
    hidx = jnp.arange(N_HEADS)[None, :, None, None]

    def one(args):
        qs, ks, vs, prow = args
        q_raw = qs.astype(F32)
        qsc = q_raw * scale
        logits = []
        if k_sel > 0:
            kp = k_pool[prow[:nbp * ppb]].astype(F32)
            kmean = jnp.mean(kp.reshape(nbp, ppb, N_HEADS, PAGE_SIZE, HEAD_DIM), axis=(1, 3))
            gate = jnp.einsum('thd,nhd->thn', q_raw, kmean)
            gate = jnp.where(jnp.asarray(np.arange(nbp)[None, :] < n_past[:, None])[:, None, :], gate, -jnp.inf)
            _, sel = lax.top_k(gate, k_sel)
            lp = sel[..., None] * ppb + jnp.arange(ppb)
            phys = prow[lp]
            n_sel = k_sel * MOBA_BLOCK
            kg = k_pool[phys, hidx].reshape(T, N_HEADS, n_sel, HEAD_DIM)
            vg = v_pool[phys, hidx].reshape(T, N_HEADS, n_sel, HEAD_DIM)
            valid = np.repeat(np.arange(k_sel)[None, :] < n_past[:, None], MOBA_BLOCK, axis=1)
            l_sel = jnp.einsum('thd,thkd->thk', qsc, kg.astype(F32))
            logits.append(jnp.where(jnp.asarray(valid)[:, None, :], l_sel, NEG))
        if n_own_pages > 0:
            ko = k_pool[prow[own_first_page:own_first_page + n_own_pages]]
            vo = v_pool[prow[own_first_page:own_first_page + n_own_pages]]
            ko = ko.transpose(1, 0, 2, 3).reshape(N_HEADS, own_past, HEAD_DIM)
            vo = vo.transpose(1, 0, 2, 3).reshape(N_HEADS, own_past, HEAD_DIM)
            logits.append(jnp.einsum('thd,hkd->thk', qsc, ko.astype(F32)))
        l_new = jnp.einsum('thd,shd->ths', qsc, ks.astype(F32))
        logits.append(jnp.where(causal_new, l_new, NEG))
        p = jax.nn.softmax(jnp.concatenate(logits, axis=-1), axis=-1)
        off = 0
        o = jnp.einsum('ths,shd->thd', p[..., p.shape[-1] - T:], vs.astype(F32))
        if k_sel > 0:
            o = o + jnp.einsum('thk,thkd->thd', p[..., :n_sel], vg.astype(F32))
            off = n_sel
        if n_own_pages > 0:
            o = o + jnp.einsum('thk,hkd->thd', p[..., off:off + own_past], vo.astype(F32))
        return o.astype(q.dtype)

    return lax.map(one, (q, k_new, v_new, page_table))


def multiscale_pool(u_ext, n_hist, pos0, w_grp, scale):
    N, L = u_ext.shape[0], u_ext.shape[1]
    T = L - n_hist
    uf = u_ext.astype(F32)
    c = jnp.concatenate([jnp.zeros_like(uf[:, :1]), lax.cumsum(uf, axis=1)], axis=1)
    t_idx = n_hist + np.arange(T)
    pos = pos0 + np.arange(T)
    hi = c[:, t_idx + 1]
    u_t = uf[:, n_hist:]
    outs = []
    for g, w in enumerate(POOL_WINDOWS):
        sl = slice(g * POOL_GW, (g + 1) * POOL_GW)
        lo = c[:, np.maximum(t_idx + 1 - w, 0), sl]
        cnt = jnp.asarray(np.minimum(pos + 1, w).astype(np.float32))[None, :, None]
        outs.append((hi[..., sl] - lo) / cnt - u_t[..., sl])
    d = jnp.stack(outs, axis=2)
    y = jnp.einsum('ntgc,gce->ntge', d, w_grp.astype(F32)).reshape(N, T, POOL_W) * scale
    return y.astype(u_ext.dtype)


def grouped_experts(h, eid, wts, w_gate, w_up, w_down):
    N, D = h.shape
    K = eid.shape[1]
    A = N * K
    e_flat = eid.reshape(A)
    tok = jnp.repeat(jnp.arange(N, dtype=jnp.int32), K)
    w_flat = wts.reshape(A)
    order = jnp.argsort(e_flat)
    e_s, tok_s, w_s = e_flat[order], tok[order], w_flat[order]
    counts = jnp.zeros((N_EXPERTS,), jnp.int32).at[e_flat].add(1)
    padded = (counts + MOE_BLOCK - 1) // MOE_BLOCK * MOE_BLOCK
    pad_end = jnp.cumsum(padded)
    pad_start = pad_end - padded
    start = jnp.cumsum(counts) - counts
    dest = pad_start[e_s] + jnp.arange(A, dtype=jnp.int32) - start[e_s]
    n_blocks = -(-(A + N_EXPERTS * (MOE_BLOCK - 1)) // MOE_BLOCK)
    n_slots = n_blocks * MOE_BLOCK
    slot_tok = jnp.full((n_slots,), N, jnp.int32).at[dest].set(tok_s)
    slot_w = jnp.zeros((n_slots,), F32).at[dest].set(w_s)
    blk_e = jnp.minimum(jnp.searchsorted(pad_end, jnp.arange(n_blocks, dtype=jnp.int32) * MOE_BLOCK, side='right'),
                        N_EXPERTS - 1).astype(jnp.int32)
    h_pad = jnp.concatenate([h, jnp.zeros((1, D), h.dtype)], axis=0)
    xb = h_pad[slot_tok].reshape(n_blocks, MOE_BLOCK, D)

    def run(args):
        e, xblk = args
        return (jax.nn.silu(xblk @ w_gate[e]) * (xblk @ w_up[e])) @ w_down[e]

    yb = lax.map(run, (blk_e, xb)).reshape(n_slots, D)
    y = jnp.zeros((N + 1, D), F32).at[slot_tok].add(yb.astype(F32) * slot_w[:, None])
    return y[:N].astype(h.dtype)


def hier_moe(h, w_rg, b_rg, w_re, b_re, w_eg, w_eu, w_ed):
    N = h.shape[0]
    hf = h.astype(F32)
    g_logits = hf @ w_rg.astype(F32) + b_rg
    g_prob = jax.nn.softmax(g_logits, axis=-1)
    g_sel = jnp.argmax(g_logits, axis=-1).astype(jnp.int32)
    p_g = jnp.take_along_axis(g_prob, g_sel[:, None], axis=-1)
    e_logits = (hf @ w_re.astype(F32) + b_re).reshape(N, N_EXPERT_GROUPS, EXPERTS_PER_GROUP)
    e_in = jnp.take_along_axis(e_logits, g_sel[:, None, None], axis=1)[:, 0]
    top_l, top_i = lax.top_k(e_in, TOPK_IN_GROUP)
    wts = jax.nn.softmax(top_l, axis=-1) * p_g
    eid = g_sel[:, None] * EXPERTS_PER_GROUP + top_i
    return grouped_experts(h, eid, wts, w_eg, w_eu, w_ed)


def finish_layer(x, attn, pooled, ga, gb, p_l, w_ao, w_po, w_o, g_f, g_p,
                 w_rg, b_rg, w_re, b_re, w_eg, w_eu, w_ed, w_pl, w_plg):
    n, t = x.shape[0], x.shape[1]
    a = attn.reshape(n, t, ATTN_W) @ w_ao
    b = pooled @ w_po
    m = jax.nn.sigmoid(ga) * a + jax.nn.sigmoid(gb) * b
    x = x + m @ w_o
    h = rmsnorm(x, g_f).reshape(n * t, D_MODEL)
    x = x + hier_moe(h, w_rg, b_rg, w_re, b_re, w_eg, w_eu, w_ed).reshape(n, t, D_MODEL)
    hp = rmsnorm(x, g_p)
    x = x + jax.nn.sigmoid(hp @ w_plg) * (p_l @ w_pl)
    return x


def setup_inputs(seed: int = 0) -> dict:
    key = jax.random.key(seed)
    ks = jax.random.split(key, 32)
    d = D_MODEL
    n_pages = PAST_LEN // PAGE_SIZE
    n_used = DEC_BATCH * n_pages
    n_pool = n_used + max(1, n_used // 4)
    nrm = lambda k, shape, s=1.0: jax.random.normal(k, shape, F32) * s
    page_table = jax.random.permutation(ks[0], n_pool)[:n_used].reshape(DEC_BATCH, n_pages).astype(jnp.int32)
    return {
        "x_prompt": nrm(ks[1], (BATCH, SEQ, d)),
        "x_sample": nrm(ks[2], (DEC_BATCH, DEC_SEQ, d)),
        "cache_k": nrm(ks[3], (DEPTH, n_pool, N_HEADS, PAGE_SIZE, HEAD_DIM)),
        "cache_v": nrm(ks[4], (DEPTH, n_pool, N_HEADS, PAGE_SIZE, HEAD_DIM)),
        "state_pool": nrm(ks[5], (DEPTH, DEC_BATCH, POOL_HIST, POOL_W)),
        "page_table": page_table,
        "p_prompt": nrm(ks[6], (DEPTH, BATCH, SEQ, PLE_DIM)),
        "p_sample": nrm(ks[7], (DEPTH, DEC_BATCH, DEC_SEQ, PLE_DIM)),
        "w_in": nrm(ks[8], (DEPTH, d, PROJ_W), d ** -0.5),
        "w_attn_out": nrm(ks[9], (DEPTH, ATTN_W, d), ATTN_W ** -0.5),
        "w_pool_out": nrm(ks[10], (DEPTH, POOL_W, d), POOL_W ** -0.5),
        "w_out": nrm(ks[11], (DEPTH, d, d), d ** -0.5),
        "w_pool_group": nrm(ks[12], (DEPTH, POOL_GROUPS, POOL_GW, POOL_GW), POOL_GW ** -0.5),
        "pool_scale": 1.0 + nrm(ks[13], (DEPTH, POOL_W), 0.02),
        "g_mix": 1.0 + nrm(ks[14], (DEPTH, d), 0.02),
        "g_ffn": 1.0 + nrm(ks[15], (DEPTH, d), 0.02),
        "g_ple": 1.0 + nrm(ks[16], (DEPTH, d), 0.02),
        "g_final": 1.0 + nrm(ks[17], (d,), 0.02),
        "w_router_group": nrm(ks[18], (DEPTH, d, N_EXPERT_GROUPS), d ** -0.5),
        "b_router_group": nrm(ks[19], (DEPTH, N_EXPERT_GROUPS), 0.01),
        "w_router_expert": nrm(ks[20], (DEPTH, d, N_EXPERTS), d ** -0.5),
        "b_router_expert": nrm(ks[21], (DEPTH, N_EXPERTS), 0.01),
        "w_exp_gate": nrm(ks[22], (DEPTH, N_EXPERTS, d, D_EXPERT), d ** -0.5),
        "w_exp_up": nrm(ks[23], (DEPTH, N_EXPERTS, d, D_EXPERT), d ** -0.5),
        "w_exp_down": nrm(ks[24], (DEPTH, N_EXPERTS, D_EXPERT, d), D_EXPERT ** -0.5),
        "w_ple": nrm(ks[25], (DEPTH, PLE_DIM, d), PLE_DIM ** -0.5),
        "w_ple_gate": nrm(ks[26], (DEPTH, d, d), d ** -0.5),
    }


def reference(x_prompt, x_sample, cache_k, cache_v, state_pool, page_table, p_prompt, p_sample,
              w_in, w_attn_out, w_pool_out, w_out, w_pool_group, pool_scale,
              g_mix, g_ffn, g_ple, g_final,
              w_router_group, b_router_group, w_router_expert, b_router_expert,
              w_exp_gate, w_exp_up, w_exp_down, w_ple, w_ple_gate):
    xp, xs = x_prompt, x_sample
    kp_rows, vp_rows, pool_p_rows = [], [], []
    ks_rows, vs_rows, pool_s_rows = [], [], []
    for i in range(DEPTH):
        post = (w_attn_out[i], w_pool_out[i], w_out[i], g_ffn[i], g_ple[i],
                w_router_group[i], b_router_group[i], w_router_expert[i], b_router_expert[i],
                w_exp_gate[i], w_exp_up[i], w_exp_down[i], w_ple[i], w_ple_gate[i])
        qp, kp, vp, up, gap, gbp = split_proj(rmsnorm(xp, g_mix[i]) @ w_in[i])
        ap = moba_prompt(qp, kp, vp)
        bp = multiscale_pool(up, 0, 0, w_pool_group[i], pool_scale[i])
        xp = finish_layer(xp, ap, bp, gap, gbp, p_prompt[i], *post)
        kp_rows.append(kp.transpose(0, 2, 1, 3))
        vp_rows.append(vp.transpose(0, 2, 1, 3))
        pool_p_rows.append(up[:, up.shape[1] - POOL_HIST:])
        qs, ks_, vs_, us, gas, gbs = split_proj(rmsnorm(xs, g_mix[i]) @ w_in[i])
        as_ = moba_sample(qs, ks_, vs_, cache_k[i], cache_v[i], page_table)
        u_ext = jnp.concatenate([state_pool[i].astype(us.dtype), us], axis=1)
        bs = multiscale_pool(u_ext, POOL_HIST, PAST_LEN, w_pool_group[i], pool_scale[i])
        xs = finish_layer(xs, as_, bs, gas, gbs, p_sample[i], *post)
        ks_rows.append(ks_.transpose(0, 2, 1, 3))
        vs_rows.append(vs_.transpose(0, 2, 1, 3))
        pool_s_rows.append(u_ext[:, u_ext.shape[1] - POOL_HIST:])
    y_prompt = rmsnorm(xp, g_final)
    y_sample = rmsnorm(xs, g_final)
    new_k_prompt = jnp.stack(kp_rows)
    new_v_prompt = jnp.stack(vp_rows)
    new_pool_prompt = jnp.stack(pool_p_rows)
    new_k_sample = jnp.stack(ks_rows)
    new_v_sample = jnp.stack(vs_rows)
    new_pool_sample = jnp.stack(pool_s_rows)
    return (y_prompt, y_sample, new_k_prompt, new_v_prompt, new_pool_prompt, new_k_sample, new_v_sample, new_pool_sample)
```

```python
import functools

import jax
import jax.numpy as jnp
from jax import lax
from jax.experimental import pallas as pl
from jax.experimental.pallas import tpu as pltpu

F32 = jnp.float32
BF16 = jnp.bfloat16

LANE = 128
SUBLANE = 8
V7X_VMEM_LIMIT_BYTES = 56 * 1024 * 1024

N_HEADS = 8
HEAD_DIM = 128
ATTN_W = N_HEADS * HEAD_DIM
MOBA_BLOCK = 256
MOBA_TOPK = 3
PAGE_SIZE = 128
POOL_WINDOWS = (2, 4, 8, 16)
POOL_GW = 256
POOL_W = len(POOL_WINDOWS) * POOL_GW
POOL_HIST = max(POOL_WINDOWS) - 1
N_EXPERT_GROUPS = 4
EXPERTS_PER_GROUP = 8
N_EXPERTS = N_EXPERT_GROUPS * EXPERTS_PER_GROUP
TOPK_IN_GROUP = 2
MOE_BLOCK = 128
RMS_EPS = 1e-6
NEG = -1e30

_NT = (((1,), (1,)), ((), ()))
_TN = (((0,), (0,)), ((), ()))
_HI = lax.Precision.HIGHEST


def _params():
    return pltpu.CompilerParams(vmem_limit_bytes=V7X_VMEM_LIMIT_BYTES)


def _rmsnorm(x, g):
    return x * lax.rsqrt(jnp.mean(x * x, axis=-1, keepdims=True) + RMS_EPS) * g


def _inproj_kernel(x_ref, g_ref, w_ref, o_ref, xn_ref):
    @pl.when(pl.program_id(1) == 0)
    def _():
        xn_ref[...] = _rmsnorm(x_ref[...], g_ref[...]).astype(BF16)

    o_ref[...] = jnp.dot(xn_ref[...], w_ref[...], preferred_element_type=F32)


def _inproj(x, g, w_bf16, *, tm=512, tn=1024):
    n, d = x.shape
    pw = w_bf16.shape[1]
    return pl.pallas_call(
        _inproj_kernel,
        grid=(n // tm, pw // tn),
        in_specs=[pl.BlockSpec((tm, d), lambda i, j: (i, 0)),
                  pl.BlockSpec((1, d), lambda i, j: (0, 0)),
                  pl.BlockSpec((d, tn), lambda i, j: (0, j))],
        out_specs=pl.BlockSpec((tm, tn), lambda i, j: (i, j)),
        out_shape=jax.ShapeDtypeStruct((n, pw), F32),
        scratch_shapes=[pltpu.VMEM((tm, d), BF16)],
        compiler_params=_params(),
        name="inproj",
    )(x, g.reshape(1, d), w_bf16)


def _moba_prompt_kernel(q_ref, k_ref, v_ref, o_ref, ko_ref, vo_ref, s_ref, *, nb):
    blk = MOBA_BLOCK
    scale = HEAD_DIM ** -0.5
    k_sel = min(MOBA_TOPK, nb - 1)
    k = k_ref[...]
    v = v_ref[...]
    ko_ref[...] = k
    vo_ref[...] = v
    kb = k.astype(BF16)
    vb = v.astype(BF16)
    kmt = jnp.concatenate(
        [jnp.broadcast_to(jnp.mean(k[j * blk:(j + 1) * blk], axis=0, keepdims=True), (LANE, HEAD_DIM))
         for j in range(nb)], axis=0)
    row = lax.broadcasted_iota(jnp.int32, (blk, blk), 0)
    col = lax.broadcasted_iota(jnp.int32, (blk, blk), 1)
    causal = col <= row
    for n in range(nb):
        q = q_ref[n * blk:(n + 1) * blk, :]
        qs = (q * scale).astype(BF16)
        sel = None
        if n > k_sel:
            gates = lax.dot_general(q, kmt[:n * LANE], _NT, precision=_HI, preferred_element_type=F32)
            g = [gates[:, j * LANE:(j + 1) * LANE] for j in range(n)]
            rank = [jnp.zeros((blk, LANE), F32) for _ in range(n)]
            for j in range(n):
                for jp in range(j):
                    first_wins = (g[jp] >= g[j]).astype(F32)
                    rank[j] = rank[j] + first_wins
                    rank[jp] = rank[jp] + (1.0 - first_wins)
            sel = [r < k_sel for r in rank]
        for j in range(n + 1):
            s = lax.dot_general(qs, kb[j * blk:(j + 1) * blk], _NT, preferred_element_type=F32)
            if j == n:
                s_ref[:, j * blk:(j + 1) * blk] = jnp.where(causal, s, NEG)
            elif sel is None:
                s_ref[:, j * blk:(j + 1) * blk] = s
            else:
                for half in range(blk // LANE):
                    lo = half * LANE
                    s_ref[:, j * blk + lo:j * blk + lo + LANE] = jnp.where(sel[j], s[:, lo:lo + LANE], NEG)
        nk = (n + 1) * blk
        s_all = s_ref[:, :nk]
        m = jnp.max(s_all, axis=-1, keepdims=True)
        p = jnp.exp(s_all - m)
        l = jnp.sum(p, axis=-1, keepdims=True)
        o = jnp.dot(p.astype(BF16), vb[:nk], preferred_element_type=F32) / l
        o_ref[n * blk:(n + 1) * blk, :] = o.astype(o_ref.dtype)


def _moba_prompt(proj, n_batch, seq):
    nb = seq // MOBA_BLOCK
    assert nb * MOBA_BLOCK == seq
    blk_spec = lambda off: pl.BlockSpec((seq, HEAD_DIM), lambda b, h: (b, off + h))
    kv_out = pl.BlockSpec((None, None, seq, HEAD_DIM), lambda b, h: (b, h, 0, 0))
    return pl.pallas_call(
        functools.partial(_moba_prompt_kernel, nb=nb),
        grid=(n_batch, N_HEADS),
        in_specs=[blk_spec(0), blk_spec(N_HEADS), blk_spec(2 * N_HEADS)],
        out_specs=[pl.BlockSpec((seq, HEAD_DIM), lambda b, h: (b, h)), kv_out, kv_out],
        out_shape=[jax.ShapeDtypeStruct((n_batch * seq, ATTN_W), BF16),
                   jax.ShapeDtypeStruct((n_batch, N_HEADS, seq, HEAD_DIM), F32),
                   jax.ShapeDtypeStruct((n_batch, N_HEADS, seq, HEAD_DIM), F32)],
        scratch_shapes=[pltpu.VMEM((MOBA_BLOCK, seq), F32)],
        compiler_params=_params(),
        name="moba_prompt",
    )(proj, proj, proj)


def _moba_sample_kernel(pt_ref, q_ref, kn_ref, vn_ref, *refs, n_pages, n_new):
    del pt_ref
    k_refs = refs[:n_pages]
    v_refs = refs[n_pages:2 * n_pages]
    o_ref = refs[2 * n_pages]
    st_ref, vcat_ref, km_ref = refs[2 * n_pages + 1:]
    ps = PAGE_SIZE
    ppb = MOBA_BLOCK // ps
    nbp = n_pages // ppb
    k_sel = min(MOBA_TOPK, nbp)
    scale = HEAD_DIM ** -0.5

    q8 = q_ref[0]
    c_i = lax.broadcasted_iota(jnp.int32, (LANE, SUBLANE), 0)
    t_i = lax.broadcasted_iota(jnp.int32, (LANE, SUBLANE), 1)
    rep = ((c_i % SUBLANE == t_i) & (c_i < N_HEADS * SUBLANE)).astype(F32)
    qrep = jnp.dot(rep, q8, precision=_HI, preferred_element_type=F32)
    cc = lax.broadcasted_iota(jnp.int32, (LANE, ATTN_W), 0)
    ll = lax.broadcasted_iota(jnp.int32, (LANE, ATTN_W), 1)
    qrep = jnp.where(ll // HEAD_DIM == cc // SUBLANE, qrep, 0.0)
    qs_b = (qrep * scale).astype(BF16)

    acc = None
    for p in range(n_pages):
        kcat = jnp.concatenate([k_refs[p][h] for h in range(N_HEADS)], axis=1)
        vcat = jnp.concatenate([v_refs[p][h] for h in range(N_HEADS)], axis=1)
        st_ref[p * ps:(p + 1) * ps, :] = lax.dot_general(kcat.astype(BF16), qs_b, _NT,
                                                         preferred_element_type=F32)
        vcat_ref[p * ps:(p + 1) * ps, :] = vcat.astype(BF16)
        colsum = jnp.sum(kcat, axis=0, keepdims=True)
        acc = colsum if p % ppb == 0 else acc + colsum
        if p % ppb == ppb - 1:
            j = p // ppb
            km_ref[j:j + 1, :] = acc * (1.0 / MOBA_BLOCK)

    gate = lax.dot_general(km_ref[...], qrep, _NT, precision=_HI, preferred_element_type=F32)
    jidx = lax.broadcasted_iota(jnp.int32, (nbp, LANE), 0)
    sel = []
    for j in range(nbp):
        gj = gate[j:j + 1, :]
        beats = (gate > gj) | ((gate == gj) & (jidx < j))
        sel.append(jnp.sum(beats.astype(F32), axis=0, keepdims=True) < k_sel)

    kn8 = kn_ref[0]
    vn8 = vn_ref[0]
    sn = lax.dot_general(kn8.astype(BF16), qs_b, _NT, preferred_element_type=F32)
    s_i = lax.broadcasted_iota(jnp.int32, (SUBLANE, LANE), 0)
    t_c = lax.broadcasted_iota(jnp.int32, (SUBLANE, LANE), 1) % SUBLANE
    sn = jnp.where((s_i <= t_c) & (s_i < n_new), sn, NEG)
    m = jnp.max(sn, axis=0, keepdims=True)
    for j in range(nbp):
        mj = jnp.max(st_ref[j * MOBA_BLOCK:(j + 1) * MOBA_BLOCK, :], axis=0, keepdims=True)
        m = jnp.where(sel[j], jnp.maximum(m, mj), m)
    pn = jnp.exp(sn - m)
    l = jnp.sum(pn, axis=0, keepdims=True)
    for j in range(nbp):
        rows = slice(j * MOBA_BLOCK, (j + 1) * MOBA_BLOCK)
        pj = jnp.where(sel[j], jnp.exp(st_ref[rows, :] - m), 0.0)
        l = l + jnp.sum(pj, axis=0, keepdims=True)
        st_ref[rows, :] = pj
    inv = 1.0 / l
    pt = (st_ref[...] * inv).astype(BF16)
    o_full = lax.dot_general(pt, vcat_ref[...], _TN, preferred_element_type=F32)
    o_full = o_full + lax.dot_general((pn * inv).astype(BF16), vn8.astype(BF16), _TN,
                                      preferred_element_type=F32)
    o_ref[0] = jnp.concatenate(
        [o_full[h * SUBLANE:(h + 1) * SUBLANE, h * HEAD_DIM:(h + 1) * HEAD_DIM] for h in range(N_HEADS)], axis=1)


def _moba_sample(q8, kn8, vn8, cache_k, cache_v, page_table, layer, n_new):
    assert n_new <= SUBLANE
    n_seq, n_pages = page_table.shape
    assert (n_pages * PAGE_SIZE) % MOBA_BLOCK == 0
    tok_spec = pl.BlockSpec((1, SUBLANE, ATTN_W), lambda s, pt: (s, 0, 0))

    def page_spec(p):
        return pl.BlockSpec((None, None, N_HEADS, PAGE_SIZE, HEAD_DIM),
                            lambda s, pt: (layer, pt[s, p], 0, 0, 0))

    past = n_pages * PAGE_SIZE
    grid_spec = pltpu.PrefetchScalarGridSpec(
        num_scalar_prefetch=1,
        grid=(n_seq,),
        in_specs=[tok_spec, tok_spec, tok_spec] + [page_spec(p) for p in range(n_pages)] * 2,
        out_specs=tok_spec,
        scratch_shapes=[pltpu.VMEM((past, LANE), F32), pltpu.VMEM((past, ATTN_W), BF16),
                        pltpu.VMEM((past // MOBA_BLOCK, ATTN_W), F32)],
    )
    return pl.pallas_call(
        functools.partial(_moba_sample_kernel, n_pages=n_pages, n_new=n_new),
        grid_spec=grid_spec,
        out_shape=jax.ShapeDtypeStruct((n_seq, SUBLANE, ATTN_W), F32),
        compiler_params=_params(),
        name="moba_sample",
    )(page_table, q8, kn8, vn8, *([cache_k] * n_pages), *([cache_v] * n_pages))


def _pool_prompt_kernel(halo_ref, u_ref, d_ref, *, tile, halo):
    t = pl.program_id(1)
    u = u_ref[...]
    prev = jnp.where(t > 0, halo_ref[...], 0.0)
    ucat = jnp.concatenate([prev, u], axis=0)
    hi = ucat.astype(BF16)
    lo = (ucat - hi.astype(F32)).astype(BF16)
    r = lax.broadcasted_iota(jnp.int32, (tile, halo + tile), 0) + halo
    c = lax.broadcasted_iota(jnp.int32, (tile, halo + tile), 1)
    pos = t * tile + lax.broadcasted_iota(jnp.int32, (tile, 1), 0)
    for g, w in enumerate(POOL_WINDOWS):
        band = ((c <= r) & (c > r - w)).astype(BF16)
        sl = slice(g * POOL_GW, (g + 1) * POOL_GW)
        acc = (jnp.dot(band, hi[:, sl], preferred_element_type=F32)
               + jnp.dot(band, lo[:, sl], preferred_element_type=F32))
        cnt = jnp.minimum(pos + 1, w).astype(F32)
        d_ref[:, sl] = (acc / cnt - u[:, sl]).astype(d_ref.dtype)


def _pool_prompt(proj, n_batch, seq, u_col, *, tile=256, halo=128):
    assert halo >= POOL_HIST and tile % halo == 0 and seq % tile == 0
    tiles = seq // tile
    per = tile // halo
    return pl.pallas_call(
        functools.partial(_pool_prompt_kernel, tile=tile, halo=halo),
        grid=(n_batch, tiles),
        in_specs=[pl.BlockSpec((halo, POOL_W), lambda b, t: (jnp.maximum((b * tiles + t) * per - 1, 0), u_col)),
                  pl.BlockSpec((tile, POOL_W), lambda b, t: (b * tiles + t, u_col))],
        out_specs=pl.BlockSpec((tile, POOL_W), lambda b, t: (b * tiles + t, 0)),
        out_shape=jax.ShapeDtypeStruct((n_batch * seq, POOL_W), BF16),
        compiler_params=_params(),
        name="pool_prompt",
    )(proj, proj)


def _pool_sample_kernel(h_ref, u_ref, d_ref, *, n_hist, n_new, pos0):
    for g, w in enumerate(POOL_WINDOWS):
        sl = slice(g * POOL_GW, (g + 1) * POOL_GW)
        rows = [h_ref[j, :, sl] for j in range(n_hist)] + [u_ref[t, :, sl] for t in range(n_new)]
        for t in range(n_new):
            ti = n_hist + t
            first = max(ti + 1 - w, 0)
            acc = rows[first]
            for j in range(first + 1, ti + 1):
                acc = acc + rows[j]
            cnt = float(min(pos0 + t + 1, w))
            d_ref[t, :, sl] = (acc / cnt - rows[ti]).astype(d_ref.dtype)


def _pool_sample(hist_t, u_t, pos0, *, chunk=32):
    n_hist, n_seq, _ = hist_t.shape
    n_new = u_t.shape[0]
    return pl.pallas_call(
        functools.partial(_pool_sample_kernel, n_hist=n_hist, n_new=n_new, pos0=pos0),
        grid=(n_seq // chunk,),
        in_specs=[pl.BlockSpec((n_hist, chunk, POOL_W), lambda i: (0, i, 0)),
                  pl.BlockSpec((n_new, chunk, POOL_W), lambda i: (0, i, 0))],
        out_specs=pl.BlockSpec((n_new, chunk, POOL_W), lambda i: (0, i, 0)),
        out_shape=jax.ShapeDtypeStruct((n_new, n_seq, POOL_W), BF16),
        compiler_params=_params(),
        name="pool_sample",
    )(hist_t, u_t)


def _merge_kernel(attn_ref, d_ref, ga_ref, gb_ref, wao_ref, wgrp_ref, wpo_ref, sc_ref, m_ref):
    a = jnp.dot(attn_ref[...], wao_ref[...], preferred_element_type=F32)
    ys = [jnp.dot(d_ref[:, g * POOL_GW:(g + 1) * POOL_GW], wgrp_ref[g], preferred_element_type=F32)
          for g in range(len(POOL_WINDOWS))]
    pooled = (jnp.concatenate(ys, axis=1) * sc_ref[...]).astype(BF16)
    b = jnp.dot(pooled, wpo_ref[...], preferred_element_type=F32)
    m = jax.nn.sigmoid(ga_ref[...]) * a + jax.nn.sigmoid(gb_ref[...]) * b
    m_ref[...] = m.astype(m_ref.dtype)


def _merge(attn, d, proj, ga_col, w_ao, w_grp, w_po, scale, *, tm=256):
    n = attn.shape[0]
    dm = w_ao.shape[1]
    const = lambda shape: pl.BlockSpec(shape, lambda i: (0,) * len(shape))
    return pl.pallas_call(
        _merge_kernel,
        grid=(n // tm,),
        in_specs=[pl.BlockSpec((tm, ATTN_W), lambda i: (i, 0)),
                  pl.BlockSpec((tm, POOL_W), lambda i: (i, 0)),
                  pl.BlockSpec((tm, dm), lambda i: (i, ga_col)),
                  pl.BlockSpec((tm, dm), lambda i: (i, ga_col + 1)),
                  const(w_ao.shape), const(w_grp.shape), const(w_po.shape), const((1, POOL_W))],
        out_specs=pl.BlockSpec((tm, dm), lambda i: (i, 0)),
        out_shape=jax.ShapeDtypeStruct((n, dm), BF16),
        compiler_params=_params(),
        name="merge",
    )(attn, d, proj, proj, w_ao, w_grp, w_po, scale.reshape(1, POOL_W))


def _first_lane_of_max(vals, lane):
    top = jnp.max(vals, axis=-1, keepdims=True)
    idx = jnp.min(jnp.where(vals == top, lane, LANE), axis=-1, keepdims=True)
    return top, idx


def _outproj_router_kernel(x_ref, m_ref, wo_ref, gf_ref, wr_ref, br_ref, x1_ref, h_ref, route_ref):
    x1 = x_ref[...] + jnp.dot(m_ref[...], wo_ref[...], preferred_element_type=F32)
    x1_ref[...] = x1
    h = _rmsnorm(x1, gf_ref[...])
    h_ref[...] = h
    logits = jnp.dot(h, wr_ref[...], precision=_HI, preferred_element_type=F32) + br_ref[...]
    lane = lax.broadcasted_iota(jnp.int32, logits.shape, 1)
    is_g = (lane >= N_EXPERTS) & (lane < N_EXPERTS + N_EXPERT_GROUPS)
    gmax, gidx = _first_lane_of_max(jnp.where(is_g, logits, -jnp.inf), lane)
    p_g = 1.0 / jnp.sum(jnp.where(is_g, jnp.exp(logits - gmax), 0.0), axis=-1, keepdims=True)
    in_grp = (lane // EXPERTS_PER_GROUP == gidx - N_EXPERTS) & (lane < N_EXPERTS)
    e1 = jnp.where(in_grp, logits, -jnp.inf)
    t1, i1 = _first_lane_of_max(e1, lane)
    t2, i2 = _first_lane_of_max(jnp.where(lane == i1, -jnp.inf, e1), lane)
    z = jnp.exp(t2 - t1)
    w1 = p_g / (1.0 + z)
    w2 = p_g * (z / (1.0 + z))
    route = jnp.where(lane == 0, i1.astype(F32),
                      jnp.where(lane == 1, i2.astype(F32),
                                jnp.where(lane == 2, w1, jnp.where(lane == 3, w2, 0.0))))
    route_ref[...] = route


def _outproj_router(x, m, w_o, g_f, w_router, b_router, *, tm=256):
    n, d = x.shape
    const = lambda shape: pl.BlockSpec(shape, lambda i: (0,) * len(shape))
    row = lambda width: pl.BlockSpec((tm, width), lambda i: (i, 0))
    return pl.pallas_call(
        _outproj_router_kernel,
        grid=(n // tm,),
        in_specs=[row(d), row(d), const(w_o.shape), const((1, d)), const(w_router.shape), const((1, LANE))],
        out_specs=[row(d), row(d), row(LANE)],
        out_shape=[jax.ShapeDtypeStruct((n, d), F32), jax.ShapeDtypeStruct((n, d), F32),
                   jax.ShapeDtypeStruct((n, LANE), F32)],
        compiler_params=_params(),
        name="outproj_router",
    )(x, m, w_o, g_f.reshape(1, d), w_router, b_router)


def _dispatch_tables(eid, wts):
    n, k = eid.shape
    a = n * k
    e_flat = eid.reshape(a)
    onehot = (e_flat[:, None] == jnp.arange(N_EXPERTS, dtype=jnp.int32)[None, :]).astype(jnp.int32)
    csum = jnp.cumsum(onehot, axis=0)
    rank = jnp.sum((csum - onehot) * onehot, axis=1)
    counts = csum[-1]
    padded = (counts + MOE_BLOCK - 1) // MOE_BLOCK * MOE_BLOCK
    pad_end = jnp.cumsum(padded)
    dest = (pad_end - padded)[e_flat] + rank
    n_blocks = -(-(a + N_EXPERTS * (MOE_BLOCK - 1)) // MOE_BLOCK)
    n_slots = n_blocks * MOE_BLOCK
    tok = jnp.arange(a, dtype=jnp.int32) // k
    slot_tok = jnp.zeros((n_slots,), jnp.int32).at[dest].set(tok)
    slot_w = jnp.zeros((n_slots,), F32).at[dest].set(wts.reshape(a))
    blk_e = jnp.minimum(jnp.searchsorted(pad_end, jnp.arange(n_blocks, dtype=jnp.int32) * MOE_BLOCK,
                                         side='right'), N_EXPERTS - 1).astype(jnp.int32)
    return dest.astype(jnp.int32), slot_tok, slot_w, blk_e


def _row_copy(src_hbm, dst_vmem, src_row, dst_row, sem):
    return pltpu.make_async_copy(src_hbm.at[pl.ds(src_row, 1)], dst_vmem.at[pl.ds(dst_row, 1)], sem)


def _moe_ffn_kernel(be_ref, st_ref, h_hbm, sw_ref, wg_ref, wu_ref, wd_ref, y_ref,
                    xbuf, sems, wg_b, wu_b, wd_b):
    j = pl.program_id(0)
    nblk = pl.num_programs(0)
    slot = j % 2

    def start_gather(blk, s):
        def body(r, carry):
            _row_copy(h_hbm, xbuf.at[s], st_ref[blk * MOE_BLOCK + r], r, sems.at[s]).start()
            return carry
        lax.fori_loop(0, MOE_BLOCK, body, 0)

    @pl.when(j == 0)
    def _():
        start_gather(0, 0)

    @pl.when(j + 1 < nblk)
    def _():
        start_gather(j + 1, 1 - slot)

    @pl.when((j == 0) | (be_ref[j] != be_ref[jnp.maximum(j - 1, 0)]))
    def _():
        wg_b[...] = wg_ref[...].astype(BF16)
        wu_b[...] = wu_ref[...].astype(BF16)
        wd_b[...] = wd_ref[...].astype(BF16)

    def wait_body(r, carry):
        _row_copy(h_hbm, xbuf.at[slot], 0, r, sems.at[slot]).wait()
        return carry
    lax.fori_loop(0, MOE_BLOCK, wait_body, 0)

    x = xbuf[slot].astype(BF16)
    gate = jnp.dot(x, wg_b[...], preferred_element_type=F32)
    up = jnp.dot(x, wu_b[...], preferred_element_type=F32)
    act = (jax.nn.silu(gate) * up).astype(BF16)
    y_ref[...] = jnp.dot(act, wd_b[...], preferred_element_type=F32) * sw_ref[...]


def _moe_ffn(h, slot_tok, slot_w, blk_e, w_gate, w_up, w_down):
    n, d = h.shape
    de = w_gate.shape[2]
    n_blocks = blk_e.shape[0]
    n_slots = n_blocks * MOE_BLOCK
    grid_spec = pltpu.PrefetchScalarGridSpec(
        num_scalar_prefetch=2,
        grid=(n_blocks,),
        in_specs=[pl.BlockSpec(memory_space=pl.ANY),
                  pl.BlockSpec((MOE_BLOCK, 1), lambda j, be, st: (j, 0)),
                  pl.BlockSpec((None, d, de), lambda j, be, st: (be[j], 0, 0)),
                  pl.BlockSpec((None, d, de), lambda j, be, st: (be[j], 0, 0)),
                  pl.BlockSpec((None, de, d), lambda j, be, st: (be[j], 0, 0))],
        out_specs=pl.BlockSpec((MOE_BLOCK, d), lambda j, be, st: (j, 0)),
        scratch_shapes=[pltpu.VMEM((2, MOE_BLOCK, d), F32), pltpu.SemaphoreType.DMA((2,)),
                        pltpu.VMEM((d, de), BF16), pltpu.VMEM((d, de), BF16), pltpu.VMEM((de, d), BF16)],
    )
    return pl.pallas_call(
        _moe_ffn_kernel,
        grid_spec=grid_spec,
        out_shape=jax.ShapeDtypeStruct((n_slots, d), F32),
        compiler_params=_params(),
        name="moe_ffn",
    )(blk_e, slot_tok, h, slot_w.reshape(n_slots, 1), w_gate, w_up, w_down)


def _combine_ple_kernel(dest_ref, x_ref, yb_hbm, p_ref, gp_ref, wplg_ref, wple_ref, gfin_ref, o_ref,
                        ybuf, sems, *, tm, final):
    i = pl.program_id(0)
    nt = pl.num_programs(0)
    slot = i % 2

    def start_gather(tile, s):
        def body(r, carry):
            tok = tile * tm + r
            for k in range(TOPK_IN_GROUP):
                _row_copy(yb_hbm, ybuf.at[s, k], dest_ref[tok * TOPK_IN_GROUP + k], r, sems.at[s]).start()
            return carry
        lax.fori_loop(0, tm, body, 0)

    @pl.when(i == 0)
    def _():
        start_gather(0, 0)

    @pl.when(i + 1 < nt)
    def _():
        start_gather(i + 1, 1 - slot)

    def wait_body(r, carry):
        for k in range(TOPK_IN_GROUP):
            _row_copy(yb_hbm, ybuf.at[slot, k], 0, r, sems.at[slot]).wait()
        return carry
    lax.fori_loop(0, tm, wait_body, 0)

    x2 = x_ref[...] + (ybuf[slot, 0] + ybuf[slot, 1])
    hp = _rmsnorm(x2, gp_ref[...]).astype(BF16)
    gate = jax.nn.sigmoid(jnp.dot(hp, wplg_ref[...], preferred_element_type=F32))
    pe = jnp.dot(p_ref[...].astype(BF16), wple_ref[...], preferred_element_type=F32)
    x3 = x2 + gate * pe
    o_ref[...] = _rmsnorm(x3, gfin_ref[...]) if final else x3


def _combine_ple(dest, x1, yb, p, g_p, w_plg, w_ple, g_final, *, final, tm=256):
    n, d = x1.shape
    pd = p.shape[1]
    const = lambda shape: pl.BlockSpec(shape, lambda i, dr: (0,) * len(shape))
    grid_spec = pltpu.PrefetchScalarGridSpec(
        num_scalar_prefetch=1,
        grid=(n // tm,),
        in_specs=[pl.BlockSpec((tm, d), lambda i, dr: (i, 0)),
                  pl.BlockSpec(memory_space=pl.ANY),
                  pl.BlockSpec((tm, pd), lambda i, dr: (i, 0)),
                  const((1, d)), const(w_plg.shape), const(w_ple.shape), const((1, d))],
        out_specs=pl.BlockSpec((tm, d), lambda i, dr: (i, 0)),
        scratch_shapes=[pltpu.VMEM((2, TOPK_IN_GROUP, tm, d), F32), pltpu.SemaphoreType.DMA((2,))],
    )
    return pl.pallas_call(
        functools.partial(_combine_ple_kernel, tm=tm, final=final),
        grid_spec=grid_spec,
        out_shape=jax.ShapeDtypeStruct((n, d), F32),
        compiler_params=_params(),
        name="combine_ple",
    )(dest, x1, yb, p, g_p.reshape(1, d), w_plg, w_ple, g_final.reshape(1, d))


def kernel(x_prompt, x_sample, cache_k, cache_v, state_pool, page_table, p_prompt, p_sample, w_in, w_attn_out, w_pool_out, w_out, w_pool_group, pool_scale, g_mix, g_ffn, g_ple, g_final, w_router_group, b_router_group, w_router_expert, b_router_expert, w_exp_gate, w_exp_up, w_exp_down, w_ple, w_ple_gate):
    nb_, seq, d = x_prompt.shape
    n_seq, n_new, _ = x_sample.shape
    depth = w_in.shape[0]
    n_p = nb_ * seq
    n_s = n_seq * n_new
    past_len = page_table.shape[1] * PAGE_SIZE
    u_col = 3 * ATTN_W // POOL_W
    ga_col = (3 * ATTN_W + POOL_W) // d
    assert 3 * ATTN_W == u_col * POOL_W and 3 * ATTN_W + POOL_W == ga_col * d

    x = jnp.concatenate([x_prompt.reshape(n_p, d), x_sample.reshape(n_s, d)], axis=0)
    outs = {name: [] for name in ("kp", "vp", "pool_p", "ks", "vs", "pool_s")}
    pad_new = lambda t: jnp.pad(t.reshape(n_seq, n_new, ATTN_W), ((0, 0), (0, SUBLANE - n_new), (0, 0)))
    heads_s = lambda t: t.reshape(n_seq, n_new, N_HEADS, HEAD_DIM).transpose(0, 2, 1, 3)

    for i in range(depth):
        proj = _inproj(x, g_mix[i], w_in[i].astype(BF16))
        attn_p, k_p, v_p = _moba_prompt(proj, nb_, seq)
        q_s, k_s, v_s = (proj[n_p:, c * ATTN_W:(c + 1) * ATTN_W] for c in range(3))
        attn_s = _moba_sample(pad_new(q_s), pad_new(k_s), pad_new(v_s), cache_k, cache_v, page_table, i, n_new)
        attn = jnp.concatenate([attn_p, attn_s[:, :n_new].reshape(n_s, ATTN_W).astype(BF16)], axis=0)
        u_p = proj[:n_p, 3 * ATTN_W:3 * ATTN_W + POOL_W].reshape(nb_, seq, POOL_W)
        u_s = proj[n_p:, 3 * ATTN_W:3 * ATTN_W + POOL_W].reshape(n_seq, n_new, POOL_W)
        d_p = _pool_prompt(proj, nb_, seq, u_col)
        d_s = _pool_sample(state_pool[i].transpose(1, 0, 2), u_s.transpose(1, 0, 2), past_len)
        dd = jnp.concatenate([d_p, d_s.transpose(1, 0, 2).reshape(n_s, POOL_W)], axis=0)
        m = _merge(attn, dd, proj, ga_col, w_attn_out[i].astype(BF16), w_pool_group[i].astype(BF16),
                   w_pool_out[i].astype(BF16), pool_scale[i])
        w_router = jnp.pad(jnp.concatenate([w_router_expert[i], w_router_group[i]], axis=1),
                           ((0, 0), (0, LANE - N_EXPERTS - N_EXPERT_GROUPS)))
        b_router = jnp.pad(jnp.concatenate([b_router_expert[i], b_router_group[i]]),
                           (0, LANE - N_EXPERTS - N_EXPERT_GROUPS)).reshape(1, LANE)
        x1, h, route = _outproj_router(x, m, w_out[i].astype(BF16), g_ffn[i], w_router, b_router)
        eid = route[:, :TOPK_IN_GROUP].astype(jnp.int32)
        wts = route[:, TOPK_IN_GROUP:2 * TOPK_IN_GROUP]
        dest, slot_tok, slot_w, blk_e = _dispatch_tables(eid, wts)
        yb = _moe_ffn(h, slot_tok, slot_w, blk_e, w_exp_gate[i], w_exp_up[i], w_exp_down[i])
        p_l = jnp.concatenate([p_prompt[i].reshape(n_p, -1), p_sample[i].reshape(n_s, -1)], axis=0)
        x = _combine_ple(dest, x1, yb, p_l, g_ple[i], w_ple_gate[i].astype(BF16), w_ple[i].astype(BF16),
                         g_final, final=(i == depth - 1))

        outs["kp"].append(k_p)
        outs["vp"].append(v_p)
        outs["pool_p"].append(u_p[:, seq - POOL_HIST:])
        outs["ks"].append(heads_s(k_s))
        outs["vs"].append(heads_s(v_s))
        outs["pool_s"].append(jnp.concatenate([state_pool[i], u_s], axis=1)[:, n_new:])

    y_prompt = x[:n_p].reshape(nb_, seq, d)
    y_sample = x[n_p:].reshape(n_seq, n_new, d)
    stack = lambda name: jnp.stack(outs[name])
    return (y_prompt, y_sample, stack("kp"), stack("vp"), stack("pool_p"),
            stack("ks"), stack("vs"), stack("pool_s"))
```

```python
import functools

import jax
import jax.numpy as jnp
from jax import lax
from jax.experimental import pallas as pl
from jax.experimental.pallas import tpu as pltpu

F32 = jnp.float32
BF16 = jnp.bfloat16

LANE = 128
SUBLANE = 8
V7X_VMEM_LIMIT_BYTES = 56 * 1024 * 1024

N_HEADS = 8
HEAD_DIM = 128
ATTN_W = N_HEADS * HEAD_DIM
MOBA_BLOCK = 256
MOBA_TOPK = 3
PAGE_SIZE = 128
POOL_WINDOWS = (2, 4, 8, 16)
POOL_GW = 256
POOL_W = len(POOL_WINDOWS) * POOL_GW
POOL_HIST = max(POOL_WINDOWS) - 1
N_EXPERT_GROUPS = 4
EXPERTS_PER_GROUP = 8
N_EXPERTS = N_EXPERT_GROUPS * EXPERTS_PER_GROUP
TOPK_IN_GROUP = 2
MOE_BLOCK = 128
RMS_EPS = 1e-6
NEG = -1e30

_NT = (((1,), (1,)), ((), ()))
_TN = (((0,), (0,)), ((), ()))
_HI = lax.Precision.HIGHEST


def _params():
    return pltpu.CompilerParams(vmem_limit_bytes=V7X_VMEM_LIMIT_BYTES)


def _rmsnorm(x, g):
    return x * lax.rsqrt(jnp.mean(x * x, axis=-1, keepdims=True) + RMS_EPS) * g


def _inproj_kernel(x_ref, g_ref, w_ref, o_ref, xn_ref):
    @pl.when(pl.program_id(1) == 0)
    def _():
        xn_ref[...] = _rmsnorm(x_ref[...], g_ref[...]).astype(BF16)

    o_ref[...] = jnp.dot(xn_ref[...], w_ref[...], preferred_element_type=F32)


def _layer_spec(shape, layer):
    return pl.BlockSpec((None,) + tuple(shape), lambda *_: (layer,) + (0,) * len(shape))


def _row_tile(n, target):
    best = SUBLANE
    for t in range(SUBLANE, target + 1, SUBLANE):
        if n % t == 0:
            best = t
    return best


def _inproj(x, g_all, w_all_bf16, layer, *, tn=1024):
    n, d = x.shape
    pw = w_all_bf16.shape[2]
    tm = _row_tile(n, 1152)
    return pl.pallas_call(
        _inproj_kernel,
        grid=(n // tm, pw // tn),
        in_specs=[pl.BlockSpec((tm, d), lambda i, j: (i, 0)),
                  _layer_spec((1, d), layer),
                  pl.BlockSpec((None, d, tn), lambda i, j: (layer, 0, j))],
        out_specs=pl.BlockSpec((tm, tn), lambda i, j: (i, j)),
        out_shape=jax.ShapeDtypeStruct((n, pw), F32),
        scratch_shapes=[pltpu.VMEM((tm, d), BF16)],
        compiler_params=_params(),
        name="inproj",
    )(x, g_all, w_all_bf16)


def _moba_prompt_kernel(q_ref, k_ref, v_ref, o_ref, ko_ref, vo_ref, s_ref, *, nb):
    blk = MOBA_BLOCK
    scale = HEAD_DIM ** -0.5
    k_sel = min(MOBA_TOPK, nb - 1)
    k = k_ref[...]
    v = v_ref[...]
    ko_ref[...] = jnp.broadcast_to(k, ko_ref.shape)
    vo_ref[...] = jnp.broadcast_to(v, vo_ref.shape)
    kb = k.astype(BF16)
    vb = v.astype(BF16)
    kmt = jnp.concatenate(
        [jnp.broadcast_to(jnp.mean(k[j * blk:(j + 1) * blk], axis=0, keepdims=True), (LANE, HEAD_DIM))
         for j in range(nb)], axis=0)
    row = lax.broadcasted_iota(jnp.int32, (blk, blk), 0)
    col = lax.broadcasted_iota(jnp.int32, (blk, blk), 1)
    causal = col <= row
    for n in range(nb):
        q = q_ref[n * blk:(n + 1) * blk, :]
        qs = (q * scale).astype(BF16)
        sel = None
        if n > k_sel:
            gates = lax.dot_general(q, kmt[:n * LANE], _NT, precision=_HI, preferred_element_type=F32)
            g = [gates[:, j * LANE:(j + 1) * LANE] for j in range(n)]
            rank = [jnp.zeros((blk, LANE), F32) for _ in range(n)]
            for j in range(n):
                for jp in range(j):
                    first_wins = (g[jp] >= g[j]).astype(F32)
                    rank[j] = rank[j] + first_wins
                    rank[jp] = rank[jp] + (1.0 - first_wins)
            sel = [r < k_sel for r in rank]
        for j in range(n + 1):
            s = lax.dot_general(qs, kb[j * blk:(j + 1) * blk], _NT, preferred_element_type=F32)
            if j == n:
                s_ref[:, j * blk:(j + 1) * blk] = jnp.where(causal, s, NEG)
            elif sel is None:
                s_ref[:, j * blk:(j + 1) * blk] = s
            else:
                for half in range(blk // LANE):
                    lo = half * LANE
                    s_ref[:, j * blk + lo:j * blk + lo + LANE] = jnp.where(sel[j], s[:, lo:lo + LANE], NEG)
        nk = (n + 1) * blk
        s_all = s_ref[:, :nk]
        m = jnp.max(s_all, axis=-1, keepdims=True)
        p = jnp.exp(s_all - m)
        l = jnp.sum(p, axis=-1, keepdims=True)
        o = jnp.dot(p.astype(BF16), vb[:nk], preferred_element_type=F32) / l
        o_ref[n * blk:(n + 1) * blk, :] = o.astype(o_ref.dtype)


def _moba_prompt_kernel_aliased(q_ref, k_ref, v_ref, kprev_ref, vprev_ref, *rest, nb):
    del kprev_ref, vprev_ref
    _moba_prompt_kernel(q_ref, k_ref, v_ref, *rest, nb=nb)


def _moba_prompt(proj, n_batch, seq, layer, depth, kv_prev):
    nb = seq // MOBA_BLOCK
    assert nb * MOBA_BLOCK == seq
    blk_spec = lambda off: pl.BlockSpec((seq, HEAD_DIM), lambda b, h: (b, off + h))
    kv_shape = jax.ShapeDtypeStruct((depth, n_batch, N_HEADS, seq, HEAD_DIM), F32)
    in_specs = [blk_spec(0), blk_spec(N_HEADS), blk_spec(2 * N_HEADS)]
    args = [proj, proj, proj]
    body, aliases = _moba_prompt_kernel, {}
    if kv_prev is None:
        kv_out = pl.BlockSpec((depth, None, None, seq, HEAD_DIM), lambda b, h: (0, b, h, 0, 0))
    else:
        kv_out = pl.BlockSpec((None, None, None, seq, HEAD_DIM), lambda b, h: (layer, b, h, 0, 0))
        in_specs += [pl.BlockSpec(memory_space=pl.ANY)] * 2
        args += list(kv_prev)
        body, aliases = _moba_prompt_kernel_aliased, {3: 1, 4: 2}
    return pl.pallas_call(
        functools.partial(body, nb=nb),
        grid=(n_batch, N_HEADS),
        in_specs=in_specs,
        out_specs=[pl.BlockSpec((seq, HEAD_DIM), lambda b, h: (b, h)), kv_out, kv_out],
        out_shape=[jax.ShapeDtypeStruct((n_batch * seq, ATTN_W), BF16), kv_shape, kv_shape],
        scratch_shapes=[pltpu.VMEM((MOBA_BLOCK, seq), F32)],
        input_output_aliases=aliases,
        compiler_params=_params(),
        name="moba_prompt",
    )(*args)


def _moba_sample_kernel(pt_ref, q_ref, kn_ref, vn_ref, *refs, n_pages, n_new):
    del pt_ref
    k_refs = refs[:n_pages]
    v_refs = refs[n_pages:2 * n_pages]
    o_ref = refs[2 * n_pages]
    st_ref, vcat_ref, km_ref = refs[2 * n_pages + 1:]
    ps = PAGE_SIZE
    ppb = MOBA_BLOCK // ps
    nbp = n_pages // ppb
    k_sel = min(MOBA_TOPK, nbp)
    scale = HEAD_DIM ** -0.5

    q8 = q_ref[0]
    c_i = lax.broadcasted_iota(jnp.int32, (LANE, SUBLANE), 0)
    t_i = lax.broadcasted_iota(jnp.int32, (LANE, SUBLANE), 1)
    rep = ((c_i % SUBLANE == t_i) & (c_i < N_HEADS * SUBLANE)).astype(F32)
    qrep = jnp.dot(rep, q8, precision=_HI, preferred_element_type=F32)
    cc = lax.broadcasted_iota(jnp.int32, (LANE, ATTN_W), 0)
    ll = lax.broadcasted_iota(jnp.int32, (LANE, ATTN_W), 1)
    qrep = jnp.where(ll // HEAD_DIM == cc // SUBLANE, qrep, 0.0)
    qs_b = (qrep * scale).astype(BF16)

    acc = None
    for p in range(n_pages):
        kcat = jnp.concatenate([k_refs[p][h] for h in range(N_HEADS)], axis=1)
        vcat = jnp.concatenate([v_refs[p][h] for h in range(N_HEADS)], axis=1)
        st_ref[p * ps:(p + 1) * ps, :] = lax.dot_general(kcat.astype(BF16), qs_b, _NT,
                                                         preferred_element_type=F32)
        vcat_ref[p * ps:(p + 1) * ps, :] = vcat.astype(BF16)
        colsum = jnp.sum(kcat, axis=0, keepdims=True)
        acc = colsum if p % ppb == 0 else acc + colsum
        if p % ppb == ppb - 1:
            j = p // ppb
            km_ref[j:j + 1, :] = acc * (1.0 / MOBA_BLOCK)

    gate = lax.dot_general(km_ref[...], qrep, _NT, precision=_HI, preferred_element_type=F32)
    jidx = lax.broadcasted_iota(jnp.int32, (nbp, LANE), 0)
    sel = []
    for j in range(nbp):
        gj = gate[j:j + 1, :]
        beats = (gate > gj) | ((gate == gj) & (jidx < j))
        sel.append(jnp.sum(beats.astype(F32), axis=0, keepdims=True) < k_sel)

    kn8 = kn_ref[0]
    vn8 = vn_ref[0]
    sn = lax.dot_general(kn8.astype(BF16), qs_b, _NT, preferred_element_type=F32)
    s_i = lax.broadcasted_iota(jnp.int32, (SUBLANE, LANE), 0)
    t_c = lax.broadcasted_iota(jnp.int32, (SUBLANE, LANE), 1) % SUBLANE
    sn = jnp.where((s_i <= t_c) & (s_i < n_new), sn, NEG)
    m = jnp.max(sn, axis=0, keepdims=True)
    for j in range(nbp):
        mj = jnp.max(st_ref[j * MOBA_BLOCK:(j + 1) * MOBA_BLOCK, :], axis=0, keepdims=True)
        m = jnp.where(sel[j], jnp.maximum(m, mj), m)
    pn = jnp.exp(sn - m)
    l = jnp.sum(pn, axis=0, keepdims=True)
    for j in range(nbp):
        rows = slice(j * MOBA_BLOCK, (j + 1) * MOBA_BLOCK)
        pj = jnp.where(sel[j], jnp.exp(st_ref[rows, :] - m), 0.0)
        l = l + jnp.sum(pj, axis=0, keepdims=True)
        st_ref[rows, :] = pj
    inv = 1.0 / l
    pt = (st_ref[...] * inv).astype(BF16)
    o_full = lax.dot_general(pt, vcat_ref[...], _TN, preferred_element_type=F32)
    o_full = o_full + lax.dot_general((pn * inv).astype(BF16), vn8.astype(BF16), _TN,
                                      preferred_element_type=F32)
    o_ref[0] = jnp.concatenate(
        [o_full[h * SUBLANE:(h + 1) * SUBLANE, h * HEAD_DIM:(h + 1) * HEAD_DIM] for h in range(N_HEADS)], axis=1)


def _moba_sample(q8, kn8, vn8, cache_k, cache_v, page_table, layer, n_new):
    assert n_new <= SUBLANE
    n_seq, n_pages = page_table.shape
    assert (n_pages * PAGE_SIZE) % MOBA_BLOCK == 0
    tok_spec = pl.BlockSpec((1, SUBLANE, ATTN_W), lambda s, pt: (s, 0, 0))

    def page_spec(p):
        return pl.BlockSpec((None, None, N_HEADS, PAGE_SIZE, HEAD_DIM),
                            lambda s, pt: (layer, pt[s, p], 0, 0, 0))

    past = n_pages * PAGE_SIZE
    grid_spec = pltpu.PrefetchScalarGridSpec(
        num_scalar_prefetch=1,
        grid=(n_seq,),
        in_specs=[tok_spec, tok_spec, tok_spec] + [page_spec(p) for p in range(n_pages)] * 2,
        out_specs=tok_spec,
        scratch_shapes=[pltpu.VMEM((past, LANE), F32), pltpu.VMEM((past, ATTN_W), BF16),
                        pltpu.VMEM((past // MOBA_BLOCK, ATTN_W), F32)],
    )
    return pl.pallas_call(
        functools.partial(_moba_sample_kernel, n_pages=n_pages, n_new=n_new),
        grid_spec=grid_spec,
        out_shape=jax.ShapeDtypeStruct((n_seq, SUBLANE, ATTN_W), F32),
        compiler_params=_params(),
        name="moba_sample",
    )(page_table, q8, kn8, vn8, *([cache_k] * n_pages), *([cache_v] * n_pages))


def _pool_prompt_kernel(halo_ref, u_ref, d_ref, *, tile, halo):
    t = pl.program_id(1)
    u = u_ref[...]
    prev = jnp.where(t > 0, halo_ref[...], 0.0)
    ucat = jnp.concatenate([prev, u], axis=0)
    hi = ucat.astype(BF16)
    lo = (ucat - hi.astype(F32)).astype(BF16)
    r = lax.broadcasted_iota(jnp.int32, (tile, halo + tile), 0) + halo
    c = lax.broadcasted_iota(jnp.int32, (tile, halo + tile), 1)
    pos = t * tile + lax.broadcasted_iota(jnp.int32, (tile, 1), 0)
    for g, w in enumerate(POOL_WINDOWS):
        band = ((c <= r) & (c > r - w)).astype(BF16)
        sl = slice(g * POOL_GW, (g + 1) * POOL_GW)
        acc = (jnp.dot(band, hi[:, sl], preferred_element_type=F32)
               + jnp.dot(band, lo[:, sl], preferred_element_type=F32))
        cnt = jnp.minimum(pos + 1, w).astype(F32)
        d_ref[:, sl] = (acc / cnt - u[:, sl]).astype(d_ref.dtype)


def _pool_prompt(proj, n_batch, seq, u_col, *, tile=256, halo=128):
    assert halo >= POOL_HIST and tile % halo == 0 and seq % tile == 0
    tiles = seq // tile
    per = tile // halo
    return pl.pallas_call(
        functools.partial(_pool_prompt_kernel, tile=tile, halo=halo),
        grid=(n_batch, tiles),
        in_specs=[pl.BlockSpec((halo, POOL_W), lambda b, t: (jnp.maximum((b * tiles + t) * per - 1, 0), u_col)),
                  pl.BlockSpec((tile, POOL_W), lambda b, t: (b * tiles + t, u_col))],
        out_specs=pl.BlockSpec((tile, POOL_W), lambda b, t: (b * tiles + t, 0)),
        out_shape=jax.ShapeDtypeStruct((n_batch * seq, POOL_W), BF16),
        compiler_params=_params(),
        name="pool_prompt",
    )(proj, proj)


def _pool_sample_kernel(h_ref, u_ref, d_ref, *, n_hist, n_new, pos0):
    for g, w in enumerate(POOL_WINDOWS):
        sl = slice(g * POOL_GW, (g + 1) * POOL_GW)
        rows = [h_ref[j, :, sl] for j in range(n_hist)] + [u_ref[t, :, sl] for t in range(n_new)]
        for t in range(n_new):
            ti = n_hist + t
            first = max(ti + 1 - w, 0)
            acc = rows[first]
            for j in range(first + 1, ti + 1):
                acc = acc + rows[j]
            cnt = float(min(pos0 + t + 1, w))
            d_ref[t, :, sl] = (acc / cnt - rows[ti]).astype(d_ref.dtype)


def _pool_sample(hist_t, u_t, pos0, *, chunk=32):
    n_hist, n_seq, _ = hist_t.shape
    n_new = u_t.shape[0]
    return pl.pallas_call(
        functools.partial(_pool_sample_kernel, n_hist=n_hist, n_new=n_new, pos0=pos0),
        grid=(n_seq // chunk,),
        in_specs=[pl.BlockSpec((n_hist, chunk, POOL_W), lambda i: (0, i, 0)),
                  pl.BlockSpec((n_new, chunk, POOL_W), lambda i: (0, i, 0))],
        out_specs=pl.BlockSpec((n_new, chunk, POOL_W), lambda i: (0, i, 0)),
        out_shape=jax.ShapeDtypeStruct((n_new, n_seq, POOL_W), BF16),
        compiler_params=_params(),
        name="pool_sample",
    )(hist_t, u_t)


def _merge_kernel(attn_ref, d_ref, ga_ref, gb_ref, wao_ref, wgrp_ref, wpo_ref, sc_ref, m_ref):
    a = jnp.dot(attn_ref[...], wao_ref[...], preferred_element_type=F32)
    ys = [jnp.dot(d_ref[:, g * POOL_GW:(g + 1) * POOL_GW], wgrp_ref[g], preferred_element_type=F32)
          for g in range(len(POOL_WINDOWS))]
    pooled = (jnp.concatenate(ys, axis=1) * sc_ref[...]).astype(BF16)
    b = jnp.dot(pooled, wpo_ref[...], preferred_element_type=F32)
    m = jax.nn.sigmoid(ga_ref[...]) * a + jax.nn.sigmoid(gb_ref[...]) * b
    m_ref[...] = m.astype(m_ref.dtype)


def _merge(attn, d, proj, ga_col, w_ao, w_grp, w_po, scale, layer, *, tm=256):
    n = attn.shape[0]
    dm = w_ao.shape[2]
    return pl.pallas_call(
        _merge_kernel,
        grid=(n // tm,),
        in_specs=[pl.BlockSpec((tm, ATTN_W), lambda i: (i, 0)),
                  pl.BlockSpec((tm, POOL_W), lambda i: (i, 0)),
                  pl.BlockSpec((tm, dm), lambda i: (i, ga_col)),
                  pl.BlockSpec((tm, dm), lambda i: (i, ga_col + 1)),
                  _layer_spec(w_ao.shape[1:], layer), _layer_spec(w_grp.shape[1:], layer),
                  _layer_spec(w_po.shape[1:], layer), _layer_spec((1, POOL_W), layer)],
        out_specs=pl.BlockSpec((tm, dm), lambda i: (i, 0)),
        out_shape=jax.ShapeDtypeStruct((n, dm), BF16),
        compiler_params=_params(),
        name="merge",
    )(attn, d, proj, proj, w_ao, w_grp, w_po, scale)


def _first_lane_of_max(vals, lane):
    top = jnp.max(vals, axis=-1, keepdims=True)
    idx = jnp.min(jnp.where(vals == top, lane, LANE), axis=-1, keepdims=True)
    return top, idx


def _one_row(ref, row):
    return ref.at[pl.ds(row, 1)]


def _outproj_router_kernel(x_ref, m_ref, wo_ref, gf_ref, wr_ref, br_ref, x1_ref, h_ref, route_ref, cnt_ref,
                           run_ref, wr3_ref):
    step = pl.program_id(0)

    d = x_ref.shape[1]

    @pl.when(step == 0)
    def _():
        run_ref[...] = jnp.zeros_like(run_ref)
        w = wr_ref[...]
        w_hi = w.astype(BF16)
        wr3_ref[0:d, :] = w_hi
        wr3_ref[d:2 * d, :] = w_hi
        wr3_ref[2 * d:3 * d, :] = (w - w_hi.astype(F32)).astype(BF16)

    x1 = x_ref[...] + jnp.dot(m_ref[...], wo_ref[...], preferred_element_type=F32)
    x1_ref[...] = x1
    h = _rmsnorm(x1, gf_ref[...])
    h_ref[...] = h
    h_hi = h.astype(BF16)
    h_lo = (h - h_hi.astype(F32)).astype(BF16)
    logits = jnp.dot(jnp.concatenate([h_hi, h_lo, h_hi], axis=1), wr3_ref[...],
                     preferred_element_type=F32) + br_ref[...]
    lane = lax.broadcasted_iota(jnp.int32, logits.shape, 1)
    is_g = (lane >= N_EXPERTS) & (lane < N_EXPERTS + N_EXPERT_GROUPS)
    gmax, gidx = _first_lane_of_max(jnp.where(is_g, logits, -jnp.inf), lane)
    p_g = 1.0 / jnp.sum(jnp.where(is_g, jnp.exp(logits - gmax), 0.0), axis=-1, keepdims=True)
    in_grp = (lane // EXPERTS_PER_GROUP == gidx - N_EXPERTS) & (lane < N_EXPERTS)
    e1 = jnp.where(in_grp, logits, -jnp.inf)
    t1, i1 = _first_lane_of_max(e1, lane)
    t2, i2 = _first_lane_of_max(jnp.where(lane == i1, -jnp.inf, e1), lane)
    z = jnp.exp(t2 - t1)
    w1 = p_g / (1.0 + z)
    w2 = p_g * (z / (1.0 + z))
    oh1 = lane == i1
    oh2 = lane == i2
    picks = (oh1 | oh2).astype(F32)
    tm = picks.shape[0]
    earlier = (lax.broadcasted_iota(jnp.int32, (tm, tm), 1)
               < lax.broadcasted_iota(jnp.int32, (tm, tm), 0)).astype(BF16)
    before = jnp.dot(earlier, picks.astype(BF16), preferred_element_type=F32) + run_ref[...]
    r1 = jnp.sum(jnp.where(oh1, before, 0.0), axis=-1, keepdims=True)
    r2 = jnp.sum(jnp.where(oh2, before, 0.0), axis=-1, keepdims=True)
    run_ref[...] = run_ref[...] + jnp.sum(picks, axis=0, keepdims=True)
    cnt_ref[...] = jnp.broadcast_to(run_ref[...], cnt_ref.shape)
    route = jnp.zeros(logits.shape, F32)
    for k, val in enumerate((i1.astype(F32), i2.astype(F32), w1, w2, r1, r2)):
        route = jnp.where(lane == k, val, route)
    route_ref[...] = route


def _outproj_router(x, m, w_o, g_f, w_router, b_router, layer, *, tm=256):
    n, d = x.shape
    row = lambda width: pl.BlockSpec((tm, width), lambda i: (i, 0))
    return pl.pallas_call(
        _outproj_router_kernel,
        grid=(n // tm,),
        in_specs=[row(d), row(d), _layer_spec((d, d), layer), _layer_spec((1, d), layer),
                  _layer_spec((d, LANE), layer), _layer_spec((1, LANE), layer)],
        out_specs=[row(d), row(d), row(LANE), pl.BlockSpec((SUBLANE, LANE), lambda i: (0, 0))],
        out_shape=[jax.ShapeDtypeStruct((n, d), F32), jax.ShapeDtypeStruct((n, d), F32),
                   jax.ShapeDtypeStruct((n, LANE), F32), jax.ShapeDtypeStruct((SUBLANE, LANE), F32)],
        scratch_shapes=[pltpu.VMEM((1, LANE), F32), pltpu.VMEM((3 * d, LANE), BF16)],
        compiler_params=_params(),
        name="outproj_router",
    )(x, m, w_o, g_f, w_router, b_router)


def _dispatch_tables(route, counts_f):
    n = route.shape[0]
    a = n * TOPK_IN_GROUP
    n_blocks = a // MOE_BLOCK
    assert n_blocks * MOE_BLOCK == a
    n_items = n_blocks + N_EXPERTS - 1
    counts = counts_f[0, :N_EXPERTS].astype(jnp.int32)
    end = jnp.cumsum(counts)
    start = end - counts
    eid = route[:, :TOPK_IN_GROUP].astype(jnp.int32)
    rank = route[:, 2 * TOPK_IN_GROUP:3 * TOPK_IN_GROUP].astype(jnp.int32)
    onehot = eid[:, :, None] == jnp.arange(N_EXPERTS, dtype=jnp.int32)[None, None, :]
    dest = (jnp.sum(jnp.where(onehot, start[None, None, :], 0), axis=-1) + rank).reshape(a)

    first_blk = start // MOE_BLOCK
    items_e = jnp.where(counts > 0, (end - 1) // MOE_BLOCK - first_blk + 1, 0)
    item_end = jnp.cumsum(items_e)
    item_off = item_end - items_e
    total = item_end[-1]
    idx = jnp.arange(n_items, dtype=jnp.int32)
    e = jnp.minimum(jnp.sum((idx[:, None] >= item_end[None, :]).astype(jnp.int32), axis=1), N_EXPERTS - 1)
    pick = lambda tbl, ee: jnp.sum(jnp.where(ee[:, None] == jnp.arange(N_EXPERTS)[None, :], tbl[None, :], 0), axis=1)
    blk = pick(first_blk, e) + idx - pick(item_off, e)
    lo = jnp.maximum(pick(start, e), blk * MOE_BLOCK) - blk * MOE_BLOCK
    hi = jnp.minimum(pick(end, e), (blk + 1) * MOE_BLOCK) - blk * MOE_BLOCK
    valid = idx < total
    e_last = jnp.sum(jnp.where(idx == total - 1, e, 0))
    e = jnp.where(valid, e, e_last)
    blk = jnp.where(valid, blk, n_blocks - 1)
    lo = jnp.where(valid, lo, 0)
    hi = jnp.where(valid, hi, 0)
    first = jnp.concatenate([jnp.ones((1,), jnp.int32), (blk[1:] != blk[:-1]).astype(jnp.int32)])
    i32 = lambda t: t.astype(jnp.int32)
    return i32(dest), i32(blk), i32(e), i32(lo), i32(hi), first


def _dispatch_kernel(dest_ref, h_ref, xs_hbm, sem, *, tm):
    base = pl.program_id(0) * tm * TOPK_IN_GROUP

    def start(r, carry):
        for k in range(TOPK_IN_GROUP):
            slot = dest_ref[base + r * TOPK_IN_GROUP + k]
            pltpu.make_async_copy(_one_row(h_ref, r), _one_row(xs_hbm, slot), sem).start(priority=k)
        return carry
    lax.fori_loop(0, tm, start, 0, unroll=8)

    def wait(r, carry):
        for k in range(TOPK_IN_GROUP):
            pltpu.make_async_copy(_one_row(h_ref, 0), _one_row(xs_hbm, 0), sem).wait()
        return carry
    lax.fori_loop(0, tm, wait, 0, unroll=8)


def _dispatch(dest, h, *, tm=256):
    n, d = h.shape
    grid_spec = pltpu.PrefetchScalarGridSpec(
        num_scalar_prefetch=1,
        grid=(n // tm,),
        in_specs=[pl.BlockSpec((tm, d), lambda i, dr: (i, 0))],
        out_specs=pl.BlockSpec(memory_space=pl.ANY),
        scratch_shapes=[pltpu.SemaphoreType.DMA(())],
    )
    return pl.pallas_call(
        functools.partial(_dispatch_kernel, tm=tm),
        grid_spec=grid_spec,
        out_shape=jax.ShapeDtypeStruct((n * TOPK_IN_GROUP, d), F32),
        compiler_params=_params(),
        name="moe_dispatch",
    )(dest, h)


def _moe_ffn_kernel(blk_ref, e_ref, lo_ref, hi_ref, first_ref, x_ref, wg_ref, wu_ref, wd_ref, y_ref,
                    wg_b, wu_b, wd_b):
    del blk_ref
    j = pl.program_id(0)

    @pl.when((j == 0) | (e_ref[j] != e_ref[jnp.maximum(j - 1, 0)]))
    def _():
        wg_b[...] = wg_ref[...].astype(BF16)
        wu_b[...] = wu_ref[...].astype(BF16)
        wd_b[...] = wd_ref[...].astype(BF16)

    @pl.when(hi_ref[j] > lo_ref[j])
    def _():
        x = x_ref[...].astype(BF16)
        gate = jnp.dot(x, wg_b[...], preferred_element_type=F32)
        up = jnp.dot(x, wu_b[...], preferred_element_type=F32)
        act = (jax.nn.silu(gate) * up).astype(BF16)
        y = jnp.dot(act, wd_b[...], preferred_element_type=F32)
        row = lax.broadcasted_iota(jnp.int32, (MOE_BLOCK, 1), 0)
        y = jnp.where((row >= lo_ref[j]) & (row < hi_ref[j]), y, 0.0)

        @pl.when(first_ref[j] == 1)
        def _():
            y_ref[...] = y

        @pl.when(first_ref[j] == 0)
        def _():
            y_ref[...] = y_ref[...] + y


def _moe_ffn(xs, blk, e, lo, hi, first, w_gate, w_up, w_down, layer):
    a, d = xs.shape
    de = w_gate.shape[3]
    n_items = blk.shape[0]
    slab = pl.BlockSpec((MOE_BLOCK, d), lambda j, blk, e, lo, hi, first: (blk[j], 0))
    wspec = lambda r, s: pl.BlockSpec((None, None, r, s), lambda j, blk, e, lo, hi, first: (layer, e[j], 0, 0))
    grid_spec = pltpu.PrefetchScalarGridSpec(
        num_scalar_prefetch=5,
        grid=(n_items,),
        in_specs=[slab, wspec(d, de), wspec(d, de), wspec(de, d)],
        out_specs=slab,
        scratch_shapes=[pltpu.VMEM((d, de), BF16), pltpu.VMEM((d, de), BF16), pltpu.VMEM((de, d), BF16)],
    )
    return pl.pallas_call(
        _moe_ffn_kernel,
        grid_spec=grid_spec,
        out_shape=jax.ShapeDtypeStruct((a, d), F32),
        compiler_params=_params(),
        name="moe_ffn",
    )(blk, e, lo, hi, first, xs, w_gate, w_up, w_down)


def _combine_ple_kernel(dest_ref, x_ref, route_ref, yb_hbm, p_ref, gp_ref, wplg_ref, wple_ref, gfin_ref, *rest,
                        tm, prompt_tiles):
    outs, (ybuf, sems) = rest[:-2], rest[-2:]
    i = pl.program_id(0)
    nt = pl.num_programs(0)
    slot = i % 2

    def start_gather(tile, s):
        def body(r, carry):
            tok = tile * tm + r
            for k in range(TOPK_IN_GROUP):
                pltpu.make_async_copy(_one_row(yb_hbm, dest_ref[tok * TOPK_IN_GROUP + k]),
                                      _one_row(ybuf.at[s, k], r), sems.at[s]).start(priority=k)
            return carry
        lax.fori_loop(0, tm, body, 0, unroll=8)

    @pl.when(i == 0)
    def _():
        start_gather(0, 0)

    @pl.when(i + 1 < nt)
    def _():
        start_gather(i + 1, 1 - slot)

    def wait_body(r, carry):
        for k in range(TOPK_IN_GROUP):
            pltpu.make_async_copy(_one_row(yb_hbm, 0), _one_row(ybuf.at[slot, k], 0), sems.at[slot]).wait()
        return carry
    lax.fori_loop(0, tm, wait_body, 0, unroll=8)

    route = route_ref[...]
    moe = sum(ybuf[slot, k] * route[:, TOPK_IN_GROUP + k:TOPK_IN_GROUP + k + 1] for k in range(TOPK_IN_GROUP))
    x2 = x_ref[...] + moe
    hp = _rmsnorm(x2, gp_ref[...]).astype(BF16)
    gate = jax.nn.sigmoid(jnp.dot(hp, wplg_ref[...], preferred_element_type=F32))
    pe = jnp.dot(p_ref[...].astype(BF16), wple_ref[...], preferred_element_type=F32)
    x3 = x2 + gate * pe
    if prompt_tiles is None:
        outs[0][...] = x3
    else:
        y = _rmsnorm(x3, gfin_ref[...])

        @pl.when(i < prompt_tiles)
        def _():
            outs[0][...] = y

        @pl.when(i >= prompt_tiles)
        def _():
            outs[1][...] = y


def _combine_ple(dest, x1, route, yb, p_all, g_p, w_plg, w_ple, g_final, layer, *, n_prompt, final, tm=256):
    n, d = x1.shape
    pd = p_all.shape[2]
    row = lambda width: pl.BlockSpec((tm, width), lambda i, dr: (i, 0))
    if final:
        assert n_prompt % tm == 0 and (n - n_prompt) % tm == 0
        pt = n_prompt // tm
        out_specs = [pl.BlockSpec((tm, d), lambda i, dr: (jnp.minimum(i, pt - 1), 0)),
                     pl.BlockSpec((tm, d), lambda i, dr: (jnp.maximum(i - pt, 0), 0))]
        out_shape = [jax.ShapeDtypeStruct((n_prompt, d), F32), jax.ShapeDtypeStruct((n - n_prompt, d), F32)]
    else:
        pt = None
        out_specs = [row(d)]
        out_shape = [jax.ShapeDtypeStruct((n, d), F32)]
    grid_spec = pltpu.PrefetchScalarGridSpec(
        num_scalar_prefetch=1,
        grid=(n // tm,),
        in_specs=[row(d), row(LANE),
                  pl.BlockSpec(memory_space=pl.ANY),
                  pl.BlockSpec((None, tm, pd), lambda i, dr: (layer, i, 0)),
                  _layer_spec((1, d), layer), _layer_spec((d, d), layer), _layer_spec((pd, d), layer),
                  pl.BlockSpec((1, d), lambda i, dr: (0, 0))],
        out_specs=out_specs,
        scratch_shapes=[pltpu.VMEM((2, TOPK_IN_GROUP, tm, d), F32), pltpu.SemaphoreType.DMA((2,))],
    )
    return pl.pallas_call(
        functools.partial(_combine_ple_kernel, tm=tm, prompt_tiles=pt),
        grid_spec=grid_spec,
        out_shape=out_shape,
        compiler_params=_params(),
        name="combine_ple",
    )(dest, x1, route, yb, p_all, g_p, w_plg, w_ple, g_final.reshape(1, d))


def kernel(x_prompt, x_sample, cache_k, cache_v, state_pool, page_table, p_prompt, p_sample, w_in, w_attn_out, w_pool_out, w_out, w_pool_group, pool_scale, g_mix, g_ffn, g_ple, g_final, w_router_group, b_router_group, w_router_expert, b_router_expert, w_exp_gate, w_exp_up, w_exp_down, w_ple, w_ple_gate):
    nb_, seq, d = x_prompt.shape
    n_seq, n_new, _ = x_sample.shape
    depth = w_in.shape[0]
    n_p = nb_ * seq
    n_s = n_seq * n_new
    past_len = page_table.shape[1] * PAGE_SIZE
    u_col = 3 * ATTN_W // POOL_W
    ga_col = (3 * ATTN_W + POOL_W) // d
    assert 3 * ATTN_W == u_col * POOL_W and 3 * ATTN_W + POOL_W == ga_col * d

    x = jnp.concatenate([x_prompt.reshape(n_p, d), x_sample.reshape(n_s, d)], axis=0)
    outs = {name: [] for name in ("pool_p", "ks", "vs", "pool_s")}
    pad_new = lambda t: jnp.pad(t.reshape(n_seq, n_new, ATTN_W), ((0, 0), (0, SUBLANE - n_new), (0, 0)))
    heads_s = lambda t: t.reshape(n_seq, n_new, N_HEADS, HEAD_DIM).transpose(0, 2, 1, 3)

    per_layer_row = lambda g: g.reshape(depth, 1, g.shape[-1])
    w_in_b, w_ao_b, w_po_b, w_o_b, w_grp_b, w_plg_b, w_ple_b = (
        w.astype(BF16) for w in (w_in, w_attn_out, w_pool_out, w_out, w_pool_group, w_ple_gate, w_ple))
    lane_pad = LANE - N_EXPERTS - N_EXPERT_GROUPS
    w_router = jnp.pad(jnp.concatenate([w_router_expert, w_router_group], axis=2), ((0, 0), (0, 0), (0, lane_pad)))
    b_router = jnp.pad(jnp.concatenate([b_router_expert, b_router_group], axis=1),
                       ((0, 0), (0, lane_pad))).reshape(depth, 1, LANE)
    p_all = jnp.concatenate([p_prompt.reshape(depth, n_p, -1), p_sample.reshape(depth, n_s, -1)], axis=1)
    kv_p = None

    for i in range(depth):
        proj = _inproj(x, per_layer_row(g_mix), w_in_b, i)
        attn_p, k_all, v_all = _moba_prompt(proj, nb_, seq, i, depth, kv_p)
        kv_p = (k_all, v_all)
        q_s, k_s, v_s = (proj[n_p:, c * ATTN_W:(c + 1) * ATTN_W] for c in range(3))
        attn_s = _moba_sample(pad_new(q_s), pad_new(k_s), pad_new(v_s), cache_k, cache_v, page_table, i, n_new)
        attn = jnp.concatenate([attn_p, attn_s[:, :n_new].reshape(n_s, ATTN_W).astype(BF16)], axis=0)
        u_p = proj[:n_p, 3 * ATTN_W:3 * ATTN_W + POOL_W].reshape(nb_, seq, POOL_W)
        u_s = proj[n_p:, 3 * ATTN_W:3 * ATTN_W + POOL_W].reshape(n_seq, n_new, POOL_W)
        d_p = _pool_prompt(proj, nb_, seq, u_col)
        d_s = _pool_sample(state_pool[i].transpose(1, 0, 2), u_s.transpose(1, 0, 2), past_len)
        dd = jnp.concatenate([d_p, d_s.transpose(1, 0, 2).reshape(n_s, POOL_W)], axis=0)
        m = _merge(attn, dd, proj, ga_col, w_ao_b, w_grp_b, w_po_b, per_layer_row(pool_scale), i)
        x1, h, route, counts = _outproj_router(x, m, w_o_b, per_layer_row(g_ffn), w_router, b_router, i)
        dest, blk, e, lo, hi, first = _dispatch_tables(route, counts)
        xs = _dispatch(dest, h)
        yb = _moe_ffn(xs, blk, e, lo, hi, first, w_exp_gate, w_exp_up, w_exp_down, i)
        res = _combine_ple(dest, x1, route, yb, p_all, per_layer_row(g_ple), w_plg_b, w_ple_b, g_final, i,
                           n_prompt=n_p, final=(i == depth - 1))
        x = res[0]

        outs["pool_p"].append(u_p[:, seq - POOL_HIST:])
        outs["ks"].append(heads_s(k_s))
        outs["vs"].append(heads_s(v_s))
        outs["pool_s"].append(jnp.concatenate([state_pool[i], u_s], axis=1)[:, n_new:])

    y_prompt = res[0].reshape(nb_, seq, d)
    y_sample = res[1].reshape(n_seq, n_new, d)
    stack = lambda name: jnp.stack(outs[name])
    return (y_prompt, y_sample, kv_p[0], kv_p[1], stack("pool_p"),
            stack("ks"), stack("vs"), stack("pool_s"))
```

```python
import functools

import jax
import jax.numpy as jnp
from jax import lax
from jax.experimental import pallas as pl
from jax.experimental.pallas import tpu as pltpu

F32 = jnp.float32
BF16 = jnp.bfloat16

LANE = 128
SUBLANE = 8
V7X_VMEM_LIMIT_BYTES = 56 * 1024 * 1024

N_HEADS = 8
HEAD_DIM = 128
ATTN_W = N_HEADS * HEAD_DIM
MOBA_BLOCK = 256
MOBA_TOPK = 3
PAGE_SIZE = 128
POOL_WINDOWS = (2, 4, 8, 16)
POOL_GW = 256
POOL_W = len(POOL_WINDOWS) * POOL_GW
POOL_HIST = max(POOL_WINDOWS) - 1
N_EXPERT_GROUPS = 4
EXPERTS_PER_GROUP = 8
N_EXPERTS = N_EXPERT_GROUPS * EXPERTS_PER_GROUP
TOPK_IN_GROUP = 2
MOE_BLOCK = 128
RMS_EPS = 1e-6
NEG = -1e30

_NT = (((1,), (1,)), ((), ()))
_TN = (((0,), (0,)), ((), ()))
_HI = lax.Precision.HIGHEST


def _params():
    return pltpu.CompilerParams(vmem_limit_bytes=V7X_VMEM_LIMIT_BYTES)


def _rmsnorm(x, g):
    return x * lax.rsqrt(jnp.mean(x * x, axis=-1, keepdims=True) + RMS_EPS) * g


def _inproj_kernel(x_ref, g_ref, w_ref, o_ref, xn_ref):
    @pl.when(pl.program_id(1) == 0)
    def _():
        xn_ref[...] = _rmsnorm(x_ref[...], g_ref[...]).astype(BF16)

    o_ref[...] = jnp.dot(xn_ref[...], w_ref[...], preferred_element_type=F32)


def _layer_spec(shape, layer):
    return pl.BlockSpec((None,) + tuple(shape), lambda *_: (layer,) + (0,) * len(shape))


def _row_tile(n, target):
    best = SUBLANE
    for t in range(SUBLANE, target + 1, SUBLANE):
        if n % t == 0:
            best = t
    return best


def _inproj(x, g_all, w_all_bf16, layer, *, tn=1024):
    n, d = x.shape
    pw = w_all_bf16.shape[2]
    tm = _row_tile(n, 1152)
    return pl.pallas_call(
        _inproj_kernel,
        grid=(n // tm, pw // tn),
        in_specs=[pl.BlockSpec((tm, d), lambda i, j: (i, 0)),
                  _layer_spec((1, d), layer),
                  pl.BlockSpec((None, d, tn), lambda i, j: (layer, 0, j))],
        out_specs=pl.BlockSpec((tm, tn), lambda i, j: (i, j)),
        out_shape=jax.ShapeDtypeStruct((n, pw), F32),
        scratch_shapes=[pltpu.VMEM((tm, d), BF16)],
        compiler_params=_params(),
        name="inproj",
    )(x, g_all, w_all_bf16)


def _moba_prompt_kernel(q_ref, k_ref, v_ref, o_ref, ko_ref, vo_ref, s_ref, *, nb):
    blk = MOBA_BLOCK
    scale = HEAD_DIM ** -0.5
    k_sel = min(MOBA_TOPK, nb - 1)
    k = k_ref[...]
    v = v_ref[...]
    ko_ref[...] = jnp.broadcast_to(k, ko_ref.shape)
    vo_ref[...] = jnp.broadcast_to(v, vo_ref.shape)
    kb = k.astype(BF16)
    vb = v.astype(BF16)
    km = jnp.concatenate([jnp.mean(k[j * blk:(j + 1) * blk], axis=0, keepdims=True) for j in range(nb)]
                         + [jnp.zeros((LANE - nb, HEAD_DIM), F32)], axis=0)
    km_hi = km.astype(BF16)
    km3 = jnp.concatenate([km_hi, km_hi, (km - km_hi.astype(F32)).astype(BF16)], axis=1)
    row = lax.broadcasted_iota(jnp.int32, (blk, blk), 0)
    col = lax.broadcasted_iota(jnp.int32, (blk, blk), 1)
    causal = col <= row
    for n in range(nb):
        q = q_ref[n * blk:(n + 1) * blk, :]
        qs = (q * scale).astype(BF16)
        sel = None
        if n > k_sel:
            q_hi = q.astype(BF16)
            q3 = jnp.concatenate([q_hi, (q - q_hi.astype(F32)).astype(BF16), q_hi], axis=1)
            gates = lax.dot_general(q3, km3, _NT, preferred_element_type=F32)
            g = [jnp.broadcast_to(gates[:, j:j + 1], (blk, LANE)) for j in range(n)]
            rank = [jnp.zeros((blk, LANE), F32) for _ in range(n)]
            for j in range(n):
                for jp in range(j):
                    first_wins = (g[jp] >= g[j]).astype(F32)
                    rank[j] = rank[j] + first_wins
                    rank[jp] = rank[jp] + (1.0 - first_wins)
            sel = [r < k_sel for r in rank]
        for j in range(n + 1):
            s = lax.dot_general(qs, kb[j * blk:(j + 1) * blk], _NT, preferred_element_type=F32)
            if j == n:
                s_ref[:, j * blk:(j + 1) * blk] = jnp.where(causal, s, NEG)
            elif sel is None:
                s_ref[:, j * blk:(j + 1) * blk] = s
            else:
                for half in range(blk // LANE):
                    lo = half * LANE
                    s_ref[:, j * blk + lo:j * blk + lo + LANE] = jnp.where(sel[j], s[:, lo:lo + LANE], NEG)
        nk = (n + 1) * blk
        s_all = s_ref[:, :nk]
        m = jnp.max(s_all, axis=-1, keepdims=True)
        p = jnp.exp(s_all - m)
        l = jnp.sum(p, axis=-1, keepdims=True)
        o = jnp.dot(p.astype(BF16), vb[:nk], preferred_element_type=F32) / l
        o_ref[n * blk:(n + 1) * blk, :] = o.astype(o_ref.dtype)


def _moba_prompt_kernel_aliased(q_ref, k_ref, v_ref, kprev_ref, vprev_ref, *rest, nb):
    del kprev_ref, vprev_ref
    _moba_prompt_kernel(q_ref, k_ref, v_ref, *rest, nb=nb)


def _moba_prompt(proj, n_batch, seq, layer, depth, kv_prev):
    nb = seq // MOBA_BLOCK
    assert nb * MOBA_BLOCK == seq
    blk_spec = lambda off: pl.BlockSpec((seq, HEAD_DIM), lambda b, h: (b, off + h))
    kv_shape = jax.ShapeDtypeStruct((depth, n_batch, N_HEADS, seq, HEAD_DIM), F32)
    in_specs = [blk_spec(0), blk_spec(N_HEADS), blk_spec(2 * N_HEADS)]
    args = [proj, proj, proj]
    body, aliases = _moba_prompt_kernel, {}
    if kv_prev is None:
        kv_out = pl.BlockSpec((depth, None, None, seq, HEAD_DIM), lambda b, h: (0, b, h, 0, 0))
    else:
        kv_out = pl.BlockSpec((None, None, None, seq, HEAD_DIM), lambda b, h: (layer, b, h, 0, 0))
        in_specs += [pl.BlockSpec(memory_space=pl.ANY)] * 2
        args += list(kv_prev)
        body, aliases = _moba_prompt_kernel_aliased, {3: 1, 4: 2}
    return pl.pallas_call(
        functools.partial(body, nb=nb),
        grid=(n_batch, N_HEADS),
        in_specs=in_specs,
        out_specs=[pl.BlockSpec((seq, HEAD_DIM), lambda b, h: (b, h)), kv_out, kv_out],
        out_shape=[jax.ShapeDtypeStruct((n_batch * seq, ATTN_W), BF16), kv_shape, kv_shape],
        scratch_shapes=[pltpu.VMEM((MOBA_BLOCK, seq), F32)],
        input_output_aliases=aliases,
        compiler_params=_params(),
        name="moba_prompt",
    )(*args)


def _moba_sample_kernel(pt_ref, q_ref, kn_ref, vn_ref, *refs, n_pages, n_new):
    del pt_ref
    k_refs = refs[:n_pages]
    v_refs = refs[n_pages:2 * n_pages]
    o_ref = refs[2 * n_pages]
    st_ref, vcat_ref, km_ref = refs[2 * n_pages + 1:]
    ps = PAGE_SIZE
    ppb = MOBA_BLOCK // ps
    nbp = n_pages // ppb
    k_sel = min(MOBA_TOPK, nbp)
    scale = HEAD_DIM ** -0.5

    q8 = q_ref[0]
    c_i = lax.broadcasted_iota(jnp.int32, (LANE, SUBLANE), 0)
    t_i = lax.broadcasted_iota(jnp.int32, (LANE, SUBLANE), 1)
    rep = ((c_i % SUBLANE == t_i) & (c_i < N_HEADS * SUBLANE)).astype(F32)
    qrep = jnp.dot(rep, q8, precision=_HI, preferred_element_type=F32)
    cc = lax.broadcasted_iota(jnp.int32, (LANE, ATTN_W), 0)
    ll = lax.broadcasted_iota(jnp.int32, (LANE, ATTN_W), 1)
    qrep = jnp.where(ll // HEAD_DIM == cc // SUBLANE, qrep, 0.0)
    qs_b = (qrep * scale).astype(BF16)

    acc = None
    for p in range(n_pages):
        kcat = jnp.concatenate([k_refs[p][h] for h in range(N_HEADS)], axis=1)
        vcat = jnp.concatenate([v_refs[p][h] for h in range(N_HEADS)], axis=1)
        st_ref[p * ps:(p + 1) * ps, :] = lax.dot_general(kcat.astype(BF16), qs_b, _NT,
                                                         preferred_element_type=F32)
        vcat_ref[p * ps:(p + 1) * ps, :] = vcat.astype(BF16)
        colsum = jnp.sum(kcat, axis=0, keepdims=True)
        acc = colsum if p % ppb == 0 else acc + colsum
        if p % ppb == ppb - 1:
            j = p // ppb
            km_ref[j:j + 1, :] = acc * (1.0 / MOBA_BLOCK)

    gate = lax.dot_general(km_ref[...], qrep, _NT, precision=_HI, preferred_element_type=F32)
    jidx = lax.broadcasted_iota(jnp.int32, (nbp, LANE), 0)
    sel = []
    for j in range(nbp):
        gj = gate[j:j + 1, :]
        beats = (gate > gj) | ((gate == gj) & (jidx < j))
        sel.append(jnp.sum(beats.astype(F32), axis=0, keepdims=True) < k_sel)

    kn8 = kn_ref[0]
    vn8 = vn_ref[0]
    sn = lax.dot_general(kn8.astype(BF16), qs_b, _NT, preferred_element_type=F32)
    s_i = lax.broadcasted_iota(jnp.int32, (SUBLANE, LANE), 0)
    t_c = lax.broadcasted_iota(jnp.int32, (SUBLANE, LANE), 1) % SUBLANE
    sn = jnp.where((s_i <= t_c) & (s_i < n_new), sn, NEG)
    m = jnp.max(sn, axis=0, keepdims=True)
    for j in range(nbp):
        mj = jnp.max(st_ref[j * MOBA_BLOCK:(j + 1) * MOBA_BLOCK, :], axis=0, keepdims=True)
        m = jnp.where(sel[j], jnp.maximum(m, mj), m)
    pn = jnp.exp(sn - m)
    l = jnp.sum(pn, axis=0, keepdims=True)
    for j in range(nbp):
        rows = slice(j * MOBA_BLOCK, (j + 1) * MOBA_BLOCK)
        pj = jnp.where(sel[j], jnp.exp(st_ref[rows, :] - m), 0.0)
        l = l + jnp.sum(pj, axis=0, keepdims=True)
        st_ref[rows, :] = pj
    inv = 1.0 / l
    pt = (st_ref[...] * inv).astype(BF16)
    o_full = lax.dot_general(pt, vcat_ref[...], _TN, preferred_element_type=F32)
    o_full = o_full + lax.dot_general((pn * inv).astype(BF16), vn8.astype(BF16), _TN,
                                      preferred_element_type=F32)
    o_ref[0] = jnp.concatenate(
        [o_full[h * SUBLANE:(h + 1) * SUBLANE, h * HEAD_DIM:(h + 1) * HEAD_DIM] for h in range(N_HEADS)], axis=1)


def _moba_sample(q8, kn8, vn8, cache_k, cache_v, page_table, layer, n_new):
    assert n_new <= SUBLANE
    n_seq, n_pages = page_table.shape
    assert (n_pages * PAGE_SIZE) % MOBA_BLOCK == 0
    tok_spec = pl.BlockSpec((1, SUBLANE, ATTN_W), lambda s, pt: (s, 0, 0))

    def page_spec(p):
        return pl.BlockSpec((None, None, N_HEADS, PAGE_SIZE, HEAD_DIM),
                            lambda s, pt: (layer, pt[s, p], 0, 0, 0))

    past = n_pages * PAGE_SIZE
    grid_spec = pltpu.PrefetchScalarGridSpec(
        num_scalar_prefetch=1,
        grid=(n_seq,),
        in_specs=[tok_spec, tok_spec, tok_spec] + [page_spec(p) for p in range(n_pages)] * 2,
        out_specs=tok_spec,
        scratch_shapes=[pltpu.VMEM((past, LANE), F32), pltpu.VMEM((past, ATTN_W), BF16),
                        pltpu.VMEM((past // MOBA_BLOCK, ATTN_W), F32)],
    )
    return pl.pallas_call(
        functools.partial(_moba_sample_kernel, n_pages=n_pages, n_new=n_new),
        grid_spec=grid_spec,
        out_shape=jax.ShapeDtypeStruct((n_seq, SUBLANE, ATTN_W), F32),
        compiler_params=_params(),
        name="moba_sample",
    )(page_table, q8, kn8, vn8, *([cache_k] * n_pages), *([cache_v] * n_pages))


def _pool_prompt_kernel(halo_ref, u_ref, d_ref, *, tile, halo):
    t = pl.program_id(1)
    u = u_ref[...]
    prev = jnp.where(t > 0, halo_ref[...], 0.0)
    ucat = jnp.concatenate([prev, u], axis=0)
    hi = ucat.astype(BF16)
    lo = (ucat - hi.astype(F32)).astype(BF16)
    r = lax.broadcasted_iota(jnp.int32, (tile, halo + tile), 0) + halo
    c = lax.broadcasted_iota(jnp.int32, (tile, halo + tile), 1)
    pos = t * tile + lax.broadcasted_iota(jnp.int32, (tile, 1), 0)
    for g, w in enumerate(POOL_WINDOWS):
        band = ((c <= r) & (c > r - w)).astype(BF16)
        sl = slice(g * POOL_GW, (g + 1) * POOL_GW)
        acc = (jnp.dot(band, hi[:, sl], preferred_element_type=F32)
               + jnp.dot(band, lo[:, sl], preferred_element_type=F32))
        cnt = jnp.minimum(pos + 1, w).astype(F32)
        d_ref[:, sl] = (acc / cnt - u[:, sl]).astype(d_ref.dtype)


def _pool_prompt(proj, n_batch, seq, u_col, *, tile=256, halo=128):
    assert halo >= POOL_HIST and tile % halo == 0 and seq % tile == 0
    tiles = seq // tile
    per = tile // halo
    return pl.pallas_call(
        functools.partial(_pool_prompt_kernel, tile=tile, halo=halo),
        grid=(n_batch, tiles),
        in_specs=[pl.BlockSpec((halo, POOL_W), lambda b, t: (jnp.maximum((b * tiles + t) * per - 1, 0), u_col)),
                  pl.BlockSpec((tile, POOL_W), lambda b, t: (b * tiles + t, u_col))],
        out_specs=pl.BlockSpec((tile, POOL_W), lambda b, t: (b * tiles + t, 0)),
        out_shape=jax.ShapeDtypeStruct((n_batch * seq, POOL_W), BF16),
        compiler_params=_params(),
        name="pool_prompt",
    )(proj, proj)


def _pool_sample_kernel(h_ref, u_ref, d_ref, *, n_hist, n_new, pos0):
    for g, w in enumerate(POOL_WINDOWS):
        sl = slice(g * POOL_GW, (g + 1) * POOL_GW)
        rows = [h_ref[j, :, sl] for j in range(n_hist)] + [u_ref[t, :, sl] for t in range(n_new)]
        for t in range(n_new):
            ti = n_hist + t
            first = max(ti + 1 - w, 0)
            acc = rows[first]
            for j in range(first + 1, ti + 1):
                acc = acc + rows[j]
            cnt = float(min(pos0 + t + 1, w))
            d_ref[t, :, sl] = (acc / cnt - rows[ti]).astype(d_ref.dtype)


def _pool_sample(hist_t, u_t, pos0, *, chunk=32):
    n_hist, n_seq, _ = hist_t.shape
    n_new = u_t.shape[0]
    return pl.pallas_call(
        functools.partial(_pool_sample_kernel, n_hist=n_hist, n_new=n_new, pos0=pos0),
        grid=(n_seq // chunk,),
        in_specs=[pl.BlockSpec((n_hist, chunk, POOL_W), lambda i: (0, i, 0)),
                  pl.BlockSpec((n_new, chunk, POOL_W), lambda i: (0, i, 0))],
        out_specs=pl.BlockSpec((n_new, chunk, POOL_W), lambda i: (0, i, 0)),
        out_shape=jax.ShapeDtypeStruct((n_new, n_seq, POOL_W), BF16),
        compiler_params=_params(),
        name="pool_sample",
    )(hist_t, u_t)


def _merge_kernel(ap_ref, as_ref, dp_ref, ds_ref, ga_ref, gb_ref, wao_ref, wgrp_ref, wpo_ref, sc_ref, m_ref, *,
                  prompt_tiles):
    is_prompt = pl.program_id(0) < prompt_tiles
    attn = jnp.where(is_prompt, ap_ref[...], as_ref[...])
    dd = jnp.where(is_prompt, dp_ref[...], ds_ref[...])
    a = jnp.dot(attn, wao_ref[...], preferred_element_type=F32)
    ys = [jnp.dot(dd[:, g * POOL_GW:(g + 1) * POOL_GW], wgrp_ref[g], preferred_element_type=F32)
          for g in range(len(POOL_WINDOWS))]
    pooled = (jnp.concatenate(ys, axis=1) * sc_ref[...]).astype(BF16)
    b = jnp.dot(pooled, wpo_ref[...], preferred_element_type=F32)
    m = jax.nn.sigmoid(ga_ref[...]) * a + jax.nn.sigmoid(gb_ref[...]) * b
    m_ref[...] = m.astype(m_ref.dtype)


def _merge(attn_p, attn_s, d_p, d_s, proj, ga_col, w_ao, w_grp, w_po, scale, layer, *, tm=256):
    n = proj.shape[0]
    dm = w_ao.shape[2]
    assert attn_p.shape[0] % tm == 0 and attn_s.shape[0] % tm == 0
    pt = attn_p.shape[0] // tm
    return pl.pallas_call(
        functools.partial(_merge_kernel, prompt_tiles=pt),
        grid=(n // tm,),
        in_specs=[*_split_rows(tm, ATTN_W, pt), *_split_rows(tm, POOL_W, pt),
                  pl.BlockSpec((tm, dm), lambda i: (i, ga_col)),
                  pl.BlockSpec((tm, dm), lambda i: (i, ga_col + 1)),
                  _layer_spec(w_ao.shape[1:], layer), _layer_spec(w_grp.shape[1:], layer),
                  _layer_spec(w_po.shape[1:], layer), _layer_spec((1, POOL_W), layer)],
        out_specs=pl.BlockSpec((tm, dm), lambda i: (i, 0)),
        out_shape=jax.ShapeDtypeStruct((n, dm), BF16),
        compiler_params=_params(),
        name="merge",
    )(attn_p, attn_s, d_p, d_s, proj, proj, w_ao, w_grp, w_po, scale)


def _first_lane_of_max(vals, lane):
    top = jnp.max(vals, axis=-1, keepdims=True)
    idx = jnp.min(jnp.where(vals == top, lane, LANE), axis=-1, keepdims=True)
    return top, idx


def _one_row(ref, row):
    return ref.at[pl.ds(row, 1)]


def _outproj_router_kernel(x_ref, m_ref, wo_ref, gf_ref, wr_ref, br_ref, x1_ref, h_ref, route_ref, cnt_ref,
                           run_ref, wr3_ref):
    step = pl.program_id(0)

    d = x_ref.shape[1]

    @pl.when(step == 0)
    def _():
        run_ref[...] = jnp.zeros_like(run_ref)
        w = wr_ref[...]
        w_hi = w.astype(BF16)
        wr3_ref[0:d, :] = w_hi
        wr3_ref[d:2 * d, :] = w_hi
        wr3_ref[2 * d:3 * d, :] = (w - w_hi.astype(F32)).astype(BF16)

    x1 = x_ref[...] + jnp.dot(m_ref[...], wo_ref[...], preferred_element_type=F32)
    x1_ref[...] = x1
    h = _rmsnorm(x1, gf_ref[...])
    h_ref[...] = h
    h_hi = h.astype(BF16)
    h_lo = (h - h_hi.astype(F32)).astype(BF16)
    logits = jnp.dot(jnp.concatenate([h_hi, h_lo, h_hi], axis=1), wr3_ref[...],
                     preferred_element_type=F32) + br_ref[...]
    lane = lax.broadcasted_iota(jnp.int32, logits.shape, 1)
    is_g = (lane >= N_EXPERTS) & (lane < N_EXPERTS + N_EXPERT_GROUPS)
    gmax, gidx = _first_lane_of_max(jnp.where(is_g, logits, -jnp.inf), lane)
    p_g = 1.0 / jnp.sum(jnp.where(is_g, jnp.exp(logits - gmax), 0.0), axis=-1, keepdims=True)
    in_grp = (lane // EXPERTS_PER_GROUP == gidx - N_EXPERTS) & (lane < N_EXPERTS)
    e1 = jnp.where(in_grp, logits, -jnp.inf)
    t1, i1 = _first_lane_of_max(e1, lane)
    t2, i2 = _first_lane_of_max(jnp.where(lane == i1, -jnp.inf, e1), lane)
    z = jnp.exp(t2 - t1)
    w1 = p_g / (1.0 + z)
    w2 = p_g * (z / (1.0 + z))
    oh1 = lane == i1
    oh2 = lane == i2
    picks = (oh1 | oh2).astype(F32)
    tm = picks.shape[0]
    earlier = (lax.broadcasted_iota(jnp.int32, (tm, tm), 1)
               < lax.broadcasted_iota(jnp.int32, (tm, tm), 0)).astype(BF16)
    before = jnp.dot(earlier, picks.astype(BF16), preferred_element_type=F32) + run_ref[...]
    r1 = jnp.sum(jnp.where(oh1, before, 0.0), axis=-1, keepdims=True)
    r2 = jnp.sum(jnp.where(oh2, before, 0.0), axis=-1, keepdims=True)
    run_ref[...] = run_ref[...] + jnp.sum(picks, axis=0, keepdims=True)
    cnt_ref[...] = jnp.broadcast_to(run_ref[...], cnt_ref.shape)
    route = jnp.zeros(logits.shape, F32)
    for k, val in enumerate((i1.astype(F32), i2.astype(F32), w1, w2, r1, r2)):
        route = jnp.where(lane == k, val, route)
    route_ref[...] = route


def _outproj_router(x, m, w_o, g_f, w_router, b_router, layer, *, tm=256):
    n, d = x.shape
    row = lambda width: pl.BlockSpec((tm, width), lambda i: (i, 0))
    return pl.pallas_call(
        _outproj_router_kernel,
        grid=(n // tm,),
        in_specs=[row(d), row(d), _layer_spec((d, d), layer), _layer_spec((1, d), layer),
                  _layer_spec((d, LANE), layer), _layer_spec((1, LANE), layer)],
        out_specs=[row(d), row(d), row(LANE), pl.BlockSpec((SUBLANE, LANE), lambda i: (0, 0))],
        out_shape=[jax.ShapeDtypeStruct((n, d), F32), jax.ShapeDtypeStruct((n, d), F32),
                   jax.ShapeDtypeStruct((n, LANE), F32), jax.ShapeDtypeStruct((SUBLANE, LANE), F32)],
        scratch_shapes=[pltpu.VMEM((1, LANE), F32), pltpu.VMEM((3 * d, LANE), BF16)],
        compiler_params=_params(),
        name="outproj_router",
    )(x, m, w_o, g_f, w_router, b_router)


def _dispatch_tables(route, counts_f):
    n = route.shape[0]
    a = n * TOPK_IN_GROUP
    n_blocks = a // MOE_BLOCK
    assert n_blocks * MOE_BLOCK == a
    n_items = n_blocks + N_EXPERTS - 1
    counts = counts_f[0, :N_EXPERTS].astype(jnp.int32)
    end = jnp.cumsum(counts)
    start = end - counts
    eid = route[:, :TOPK_IN_GROUP].astype(jnp.int32)
    rank = route[:, 2 * TOPK_IN_GROUP:3 * TOPK_IN_GROUP].astype(jnp.int32)
    onehot = eid[:, :, None] == jnp.arange(N_EXPERTS, dtype=jnp.int32)[None, None, :]
    dest = (jnp.sum(jnp.where(onehot, start[None, None, :], 0), axis=-1) + rank).reshape(a)

    first_blk = start // MOE_BLOCK
    items_e = jnp.where(counts > 0, (end - 1) // MOE_BLOCK - first_blk + 1, 0)
    item_end = jnp.cumsum(items_e)
    item_off = item_end - items_e
    total = item_end[-1]
    idx = jnp.arange(n_items, dtype=jnp.int32)
    e = jnp.minimum(jnp.sum((idx[:, None] >= item_end[None, :]).astype(jnp.int32), axis=1), N_EXPERTS - 1)
    pick = lambda tbl, ee: jnp.sum(jnp.where(ee[:, None] == jnp.arange(N_EXPERTS)[None, :], tbl[None, :], 0), axis=1)
    blk = pick(first_blk, e) + idx - pick(item_off, e)
    lo = jnp.maximum(pick(start, e), blk * MOE_BLOCK) - blk * MOE_BLOCK
    hi = jnp.minimum(pick(end, e), (blk + 1) * MOE_BLOCK) - blk * MOE_BLOCK
    valid = idx < total
    e_last = jnp.sum(jnp.where(idx == total - 1, e, 0))
    e = jnp.where(valid, e, e_last)
    blk = jnp.where(valid, blk, n_blocks - 1)
    lo = jnp.where(valid, lo, 0)
    hi = jnp.where(valid, hi, 0)
    first = jnp.concatenate([jnp.ones((1,), jnp.int32), (blk[1:] != blk[:-1]).astype(jnp.int32)])
    ids = jnp.arange(N_EXPERTS, dtype=jnp.int32)
    later = (ids[None, :] > ids[:, None]) & (counts[None, :] > 0)
    nxt_e = jnp.min(jnp.where(later, ids[None, :], N_EXPERTS), axis=1)
    nxt = pick(jnp.where(nxt_e < N_EXPERTS, nxt_e, -1), e)
    i32 = lambda t: t.astype(jnp.int32)
    return i32(dest), i32(blk), i32(e), i32(lo), i32(hi), first, i32(nxt)


def _dispatch_kernel(dest_ref, h_ref, xs_hbm, sem, *, tm):
    base = pl.program_id(0) * tm * TOPK_IN_GROUP

    def start(g, carry):
        r0 = pl.multiple_of(g * SUBLANE, SUBLANE)
        for u in range(SUBLANE):
            for k in range(TOPK_IN_GROUP):
                slot = dest_ref[base + (r0 + u) * TOPK_IN_GROUP + k]
                pltpu.make_async_copy(_one_row(h_ref, r0 + u), _one_row(xs_hbm, slot), sem).start(priority=k)
        return carry
    lax.fori_loop(0, tm // SUBLANE, start, 0)

    def wait(r, carry):
        for k in range(TOPK_IN_GROUP):
            pltpu.make_async_copy(_one_row(h_ref, 0), _one_row(xs_hbm, 0), sem).wait()
        return carry
    lax.fori_loop(0, tm, wait, 0, unroll=8)


def _dispatch(dest, h, *, tm=256):
    n, d = h.shape
    grid_spec = pltpu.PrefetchScalarGridSpec(
        num_scalar_prefetch=1,
        grid=(n // tm,),
        in_specs=[pl.BlockSpec((tm, d), lambda i, dr: (i, 0))],
        out_specs=pl.BlockSpec(memory_space=pl.ANY),
        scratch_shapes=[pltpu.SemaphoreType.DMA(())],
    )
    return pl.pallas_call(
        functools.partial(_dispatch_kernel, tm=tm),
        grid_spec=grid_spec,
        out_shape=jax.ShapeDtypeStruct((n * TOPK_IN_GROUP, d), F32),
        compiler_params=_params(),
        name="moe_dispatch",
    )(dest, h)


def _moe_ffn_kernel(blk_ref, e_ref, lo_ref, hi_ref, first_ref, nxt_ref, x_ref, wg_hbm, wu_hbm, wd_hbm, y_ref,
                    stage_g, stage_u, stage_d, wg_b, wu_b, wd_b, sems, cur_ref, *, layer):
    del blk_ref
    j = pl.program_id(0)

    def weight_copies(expert, s):
        return (pltpu.make_async_copy(wg_hbm.at[layer, expert], stage_g.at[s], sems.at[s]),
                pltpu.make_async_copy(wu_hbm.at[layer, expert], stage_u.at[s], sems.at[s]),
                pltpu.make_async_copy(wd_hbm.at[layer, expert], stage_d.at[s], sems.at[s]))

    @pl.when(j == 0)
    def _():
        cur_ref[0] = 1
        for c in weight_copies(e_ref[0], 0):
            c.start()

    @pl.when((j == 0) | (e_ref[j] != e_ref[jnp.maximum(j - 1, 0)]))
    def _():
        s = 1 - cur_ref[0]
        cur_ref[0] = s

        @pl.when(nxt_ref[j] >= 0)
        def _():
            for c in weight_copies(nxt_ref[j], 1 - s):
                c.start()

        for c in weight_copies(e_ref[j], s):
            c.wait()
        wg_b[...] = stage_g[s].astype(BF16)
        wu_b[...] = stage_u[s].astype(BF16)
        wd_b[...] = stage_d[s].astype(BF16)

    @pl.when(hi_ref[j] > lo_ref[j])
    def _():
        x = x_ref[...].astype(BF16)
        gate = jnp.dot(x, wg_b[...], preferred_element_type=F32)
        up = jnp.dot(x, wu_b[...], preferred_element_type=F32)
        act = (jax.nn.silu(gate) * up).astype(BF16)
        y = jnp.dot(act, wd_b[...], preferred_element_type=F32)
        row = lax.broadcasted_iota(jnp.int32, (MOE_BLOCK, 1), 0)
        y = jnp.where((row >= lo_ref[j]) & (row < hi_ref[j]), y, 0.0)

        @pl.when(first_ref[j] == 1)
        def _():
            y_ref[...] = y

        @pl.when(first_ref[j] == 0)
        def _():
            y_ref[...] = y_ref[...] + y


def _moe_ffn(xs, blk, e, lo, hi, first, nxt, w_gate, w_up, w_down, layer):
    a, d = xs.shape
    de = w_gate.shape[3]
    n_items = blk.shape[0]
    slab = pl.BlockSpec((MOE_BLOCK, d), lambda j, blk, *_: (blk[j], 0))
    hbm = pl.BlockSpec(memory_space=pl.ANY)
    grid_spec = pltpu.PrefetchScalarGridSpec(
        num_scalar_prefetch=6,
        grid=(n_items,),
        in_specs=[slab, hbm, hbm, hbm],
        out_specs=slab,
        scratch_shapes=[pltpu.VMEM((2, d, de), F32), pltpu.VMEM((2, d, de), F32), pltpu.VMEM((2, de, d), F32),
                        pltpu.VMEM((d, de), BF16), pltpu.VMEM((d, de), BF16), pltpu.VMEM((de, d), BF16),
                        pltpu.SemaphoreType.DMA((2,)), pltpu.SMEM((1,), jnp.int32)],
    )
    return pl.pallas_call(
        functools.partial(_moe_ffn_kernel, layer=layer),
        grid_spec=grid_spec,
        out_shape=jax.ShapeDtypeStruct((a, d), F32),
        compiler_params=_params(),
        name="moe_ffn",
    )(blk, e, lo, hi, first, nxt, xs, w_gate, w_up, w_down)


def _combine_ple_kernel(dest_ref, x_ref, route_ref, yb_hbm, pp_ref, ps_ref, gp_ref, wplg_ref, wple_ref, gfin_ref,
                        *rest, tm, prompt_tiles, final):
    outs, (ybuf, sems) = rest[:-2], rest[-2:]
    i = pl.program_id(0)
    nt = pl.num_programs(0)
    slot = i % 2

    def start_gather(tile, s):
        def body(g, carry):
            r0 = pl.multiple_of(g * SUBLANE, SUBLANE)
            for u in range(SUBLANE):
                for k in range(TOPK_IN_GROUP):
                    src = dest_ref[(tile * tm + r0 + u) * TOPK_IN_GROUP + k]
                    pltpu.make_async_copy(_one_row(yb_hbm, src), _one_row(ybuf.at[s, k], r0 + u),
                                          sems.at[s]).start(priority=k)
            return carry
        lax.fori_loop(0, tm // SUBLANE, body, 0)

    @pl.when(i == 0)
    def _():
        start_gather(0, 0)

    @pl.when(i + 1 < nt)
    def _():
        start_gather(i + 1, 1 - slot)

    def wait_body(r, carry):
        for k in range(TOPK_IN_GROUP):
            pltpu.make_async_copy(_one_row(yb_hbm, 0), _one_row(ybuf.at[slot, k], 0), sems.at[slot]).wait()
        return carry
    lax.fori_loop(0, tm, wait_body, 0, unroll=8)

    route = route_ref[...]
    moe = sum(ybuf[slot, k] * route[:, TOPK_IN_GROUP + k:TOPK_IN_GROUP + k + 1] for k in range(TOPK_IN_GROUP))
    x2 = x_ref[...] + moe
    hp = _rmsnorm(x2, gp_ref[...]).astype(BF16)
    gate = jax.nn.sigmoid(jnp.dot(hp, wplg_ref[...], preferred_element_type=F32))
    p = jnp.where(i < prompt_tiles, pp_ref[...], ps_ref[...])
    pe = jnp.dot(p.astype(BF16), wple_ref[...], preferred_element_type=F32)
    x3 = x2 + gate * pe
    if not final:
        outs[0][...] = x3
    else:
        y = _rmsnorm(x3, gfin_ref[...])

        @pl.when(i < prompt_tiles)
        def _():
            outs[0][...] = y

        @pl.when(i >= prompt_tiles)
        def _():
            outs[1][...] = y


def _split_rows(tm, width, pt, lead=()):
    none = (None,) * len(lead)
    return (pl.BlockSpec(none + (tm, width), lambda i, *_: lead + (jnp.minimum(i, pt - 1), 0)),
            pl.BlockSpec(none + (tm, width), lambda i, *_: lead + (jnp.maximum(i - pt, 0), 0)))


def _combine_ple(dest, x1, route, yb, p_prompt, p_sample, g_p, w_plg, w_ple, g_final, layer, *, final, tm=256):
    n, d = x1.shape
    n_prompt, pd = p_prompt.shape[1:]
    assert n_prompt % tm == 0 and (n - n_prompt) % tm == 0
    pt = n_prompt // tm
    row = lambda width: pl.BlockSpec((tm, width), lambda i, dr: (i, 0))
    if final:
        out_specs = list(_split_rows(tm, d, pt))
        out_shape = [jax.ShapeDtypeStruct((n_prompt, d), F32), jax.ShapeDtypeStruct((n - n_prompt, d), F32)]
    else:
        out_specs = [row(d)]
        out_shape = [jax.ShapeDtypeStruct((n, d), F32)]
    grid_spec = pltpu.PrefetchScalarGridSpec(
        num_scalar_prefetch=1,
        grid=(n // tm,),
        in_specs=[row(d), row(LANE), pl.BlockSpec(memory_space=pl.ANY), *_split_rows(tm, pd, pt, (layer,)),
                  _layer_spec((1, d), layer), _layer_spec((d, d), layer), _layer_spec((pd, d), layer),
                  pl.BlockSpec((1, d), lambda i, dr: (0, 0))],
        out_specs=out_specs,
        scratch_shapes=[pltpu.VMEM((2, TOPK_IN_GROUP, tm, d), F32), pltpu.SemaphoreType.DMA((2,))],
    )
    return pl.pallas_call(
        functools.partial(_combine_ple_kernel, tm=tm, prompt_tiles=pt, final=final),
        grid_spec=grid_spec,
        out_shape=out_shape,
        compiler_params=_params(),
        name="combine_ple",
    )(dest, x1, route, yb, p_prompt, p_sample, g_p, w_plg, w_ple, g_final.reshape(1, d))


def kernel(x_prompt, x_sample, cache_k, cache_v, state_pool, page_table, p_prompt, p_sample, w_in, w_attn_out, w_pool_out, w_out, w_pool_group, pool_scale, g_mix, g_ffn, g_ple, g_final, w_router_group, b_router_group, w_router_expert, b_router_expert, w_exp_gate, w_exp_up, w_exp_down, w_ple, w_ple_gate):
    nb_, seq, d = x_prompt.shape
    n_seq, n_new, _ = x_sample.shape
    depth = w_in.shape[0]
    n_p = nb_ * seq
    n_s = n_seq * n_new
    past_len = page_table.shape[1] * PAGE_SIZE
    u_col = 3 * ATTN_W // POOL_W
    ga_col = (3 * ATTN_W + POOL_W) // d
    assert 3 * ATTN_W == u_col * POOL_W and 3 * ATTN_W + POOL_W == ga_col * d

    x = jnp.concatenate([x_prompt.reshape(n_p, d), x_sample.reshape(n_s, d)], axis=0)
    outs = {name: [] for name in ("pool_p", "ks", "vs", "pool_s")}
    pad_new = lambda t: jnp.pad(t.reshape(n_seq, n_new, ATTN_W), ((0, 0), (0, SUBLANE - n_new), (0, 0)))
    heads_s = lambda t: t.reshape(n_seq, n_new, N_HEADS, HEAD_DIM).transpose(0, 2, 1, 3)

    per_layer_row = lambda g: g.reshape(depth, 1, g.shape[-1])
    w_in_b, w_ao_b, w_po_b, w_o_b, w_grp_b, w_plg_b, w_ple_b = (
        w.astype(BF16) for w in (w_in, w_attn_out, w_pool_out, w_out, w_pool_group, w_ple_gate, w_ple))
    lane_pad = LANE - N_EXPERTS - N_EXPERT_GROUPS
    w_router = jnp.pad(jnp.concatenate([w_router_expert, w_router_group], axis=2), ((0, 0), (0, 0), (0, lane_pad)))
    b_router = jnp.pad(jnp.concatenate([b_router_expert, b_router_group], axis=1),
                       ((0, 0), (0, lane_pad))).reshape(depth, 1, LANE)
    pp, ps = p_prompt.reshape(depth, n_p, -1), p_sample.reshape(depth, n_s, -1)
    kv_p = None

    for i in range(depth):
        proj = _inproj(x, per_layer_row(g_mix), w_in_b, i)
        attn_p, k_all, v_all = _moba_prompt(proj, nb_, seq, i, depth, kv_p)
        kv_p = (k_all, v_all)
        q_s, k_s, v_s = (proj[n_p:, c * ATTN_W:(c + 1) * ATTN_W] for c in range(3))
        attn_s = _moba_sample(pad_new(q_s), pad_new(k_s), pad_new(v_s), cache_k, cache_v, page_table, i, n_new)
        attn_s = attn_s[:, :n_new].reshape(n_s, ATTN_W).astype(BF16)
        u_p = proj[:n_p, 3 * ATTN_W:3 * ATTN_W + POOL_W].reshape(nb_, seq, POOL_W)
        u_s = proj[n_p:, 3 * ATTN_W:3 * ATTN_W + POOL_W].reshape(n_seq, n_new, POOL_W)
        d_p = _pool_prompt(proj, nb_, seq, u_col)
        d_s = _pool_sample(state_pool[i].transpose(1, 0, 2), u_s.transpose(1, 0, 2), past_len)
        d_s = d_s.transpose(1, 0, 2).reshape(n_s, POOL_W)
        m = _merge(attn_p, attn_s, d_p, d_s, proj, ga_col, w_ao_b, w_grp_b, w_po_b, per_layer_row(pool_scale), i)
        x1, h, route, counts = _outproj_router(x, m, w_o_b, per_layer_row(g_ffn), w_router, b_router, i)
        dest, blk, e, lo, hi, first, nxt = _dispatch_tables(route, counts)
        xs = _dispatch(dest, h)
        yb = _moe_ffn(xs, blk, e, lo, hi, first, nxt, w_exp_gate, w_exp_up, w_exp_down, i)
        res = _combine_ple(dest, x1, route, yb, pp, ps, per_layer_row(g_ple), w_plg_b, w_ple_b, g_final, i,
                           final=(i == depth - 1))
        x = res[0]

        outs["pool_p"].append(u_p[:, seq - POOL_HIST:])
        outs["ks"].append(heads_s(k_s))
        outs["vs"].append(heads_s(v_s))
        outs["pool_s"].append(jnp.concatenate([state_pool[i], u_s], axis=1)[:, n_new:])

    y_prompt = res[0].reshape(nb_, seq, d)
    y_sample = res[1].reshape(n_seq, n_new, d)
    stack = lambda name: jnp.stack(outs[name])
    return (y_prompt, y_sample, kv_p[0], kv_p[1], stack("pool_p"),
            stack("ks"), stack("vs"), stack("pool_s"))
```

```python
import functools

import jax
import jax.numpy as jnp
from jax import lax
from jax.experimental import pallas as pl
from jax.experimental.pallas import tpu as pltpu

F32 = jnp.float32
BF16 = jnp.bfloat16

LANE = 128
SUBLANE = 8
V7X_VMEM_LIMIT_BYTES = 56 * 1024 * 1024

N_HEADS = 8
HEAD_DIM = 128
ATTN_W = N_HEADS * HEAD_DIM
MOBA_BLOCK = 256
MOBA_TOPK = 3
PAGE_SIZE = 128
POOL_WINDOWS = (2, 4, 8, 16)
POOL_GW = 256
POOL_W = len(POOL_WINDOWS) * POOL_GW
POOL_HIST = max(POOL_WINDOWS) - 1
N_EXPERT_GROUPS = 4
EXPERTS_PER_GROUP = 8
N_EXPERTS = N_EXPERT_GROUPS * EXPERTS_PER_GROUP
TOPK_IN_GROUP = 2
MOE_BLOCK = 128
RMS_EPS = 1e-6
NEG = -1e30

_NT = (((1,), (1,)), ((), ()))
_TN = (((0,), (0,)), ((), ()))
_HI = lax.Precision.HIGHEST


def _params():
    return pltpu.CompilerParams(vmem_limit_bytes=V7X_VMEM_LIMIT_BYTES)


def _rmsnorm(x, g):
    return x * lax.rsqrt(jnp.mean(x * x, axis=-1, keepdims=True) + RMS_EPS) * g


def _inproj_kernel(x_ref, g_ref, w_ref, o_ref, xn_ref):
    @pl.when(pl.program_id(1) == 0)
    def _():
        xn_ref[...] = _rmsnorm(x_ref[...], g_ref[...]).astype(BF16)

    o_ref[...] = jnp.dot(xn_ref[...], w_ref[...], preferred_element_type=F32)


def _layer_spec(shape, layer):
    return pl.BlockSpec((None,) + tuple(shape), lambda *_: (layer,) + (0,) * len(shape))


def _row_tile(n, target):
    best = SUBLANE
    for t in range(SUBLANE, target + 1, SUBLANE):
        if n % t == 0:
            best = t
    return best


def _inproj(x, g_all, w_all_bf16, layer, *, tn=1024):
    n, d = x.shape
    pw = w_all_bf16.shape[2]
    tm = _row_tile(n, 1152)
    return pl.pallas_call(
        _inproj_kernel,
        grid=(n // tm, pw // tn),
        in_specs=[pl.BlockSpec((tm, d), lambda i, j: (i, 0)),
                  _layer_spec((1, d), layer),
                  pl.BlockSpec((None, d, tn), lambda i, j: (layer, 0, j))],
        out_specs=pl.BlockSpec((tm, tn), lambda i, j: (i, j)),
        out_shape=jax.ShapeDtypeStruct((n, pw), F32),
        scratch_shapes=[pltpu.VMEM((tm, d), BF16)],
        compiler_params=_params(),
        name="inproj",
    )(x, g_all, w_all_bf16)


def _moba_prompt_kernel(q_ref, k_ref, v_ref, o_ref, ko_ref, vo_ref, s_ref, *, nb):
    blk = MOBA_BLOCK
    scale = HEAD_DIM ** -0.5
    k_sel = min(MOBA_TOPK, nb - 1)
    k = k_ref[...]
    v = v_ref[...]
    ko_ref[...] = jnp.broadcast_to(k, ko_ref.shape)
    vo_ref[...] = jnp.broadcast_to(v, vo_ref.shape)
    kb = k.astype(BF16)
    vb = v.astype(BF16)
    km = jnp.concatenate([jnp.mean(k[j * blk:(j + 1) * blk], axis=0, keepdims=True) for j in range(nb)]
                         + [jnp.zeros((LANE - nb, HEAD_DIM), F32)], axis=0)
    km_hi = km.astype(BF16)
    km3 = jnp.concatenate([km_hi, km_hi, (km - km_hi.astype(F32)).astype(BF16)], axis=1)
    row = lax.broadcasted_iota(jnp.int32, (blk, blk), 0)
    col = lax.broadcasted_iota(jnp.int32, (blk, blk), 1)
    causal = col <= row
    for n in range(nb):
        q = q_ref[n * blk:(n + 1) * blk, :]
        qs = (q * scale).astype(BF16)
        sel = None
        if n > k_sel:
            q_hi = q.astype(BF16)
            q3 = jnp.concatenate([q_hi, (q - q_hi.astype(F32)).astype(BF16), q_hi], axis=1)
            gates = lax.dot_general(q3, km3, _NT, preferred_element_type=F32)
            g = [jnp.broadcast_to(gates[:, j:j + 1], (blk, LANE)) for j in range(n)]
            rank = [jnp.zeros((blk, LANE), F32) for _ in range(n)]
            for j in range(n):
                for jp in range(j):
                    first_wins = (g[jp] >= g[j]).astype(F32)
                    rank[j] = rank[j] + first_wins
                    rank[jp] = rank[jp] + (1.0 - first_wins)
            sel = [r < k_sel for r in rank]
        for j in range(n + 1):
            s = lax.dot_general(qs, kb[j * blk:(j + 1) * blk], _NT, preferred_element_type=F32)
            if j == n:
                s_ref[:, j * blk:(j + 1) * blk] = jnp.where(causal, s, NEG)
            elif sel is None:
                s_ref[:, j * blk:(j + 1) * blk] = s
            else:
                for half in range(blk // LANE):
                    lo = half * LANE
                    s_ref[:, j * blk + lo:j * blk + lo + LANE] = jnp.where(sel[j], s[:, lo:lo + LANE], NEG)
        nk = (n + 1) * blk
        s_all = s_ref[:, :nk]
        m = jnp.max(s_all, axis=-1, keepdims=True)
        p = jnp.exp(s_all - m)
        l = jnp.sum(p, axis=-1, keepdims=True)
        o = jnp.dot(p.astype(BF16), vb[:nk], preferred_element_type=F32) / l
        o_ref[n * blk:(n + 1) * blk, :] = o.astype(o_ref.dtype)


def _moba_prompt_kernel_aliased(q_ref, k_ref, v_ref, kprev_ref, vprev_ref, *rest, nb):
    del kprev_ref, vprev_ref
    _moba_prompt_kernel(q_ref, k_ref, v_ref, *rest, nb=nb)


def _moba_prompt(proj, n_batch, seq, layer, depth, kv_prev):
    nb = seq // MOBA_BLOCK
    assert nb * MOBA_BLOCK == seq
    blk_spec = lambda off: pl.BlockSpec((seq, HEAD_DIM), lambda b, h: (b, off + h))
    kv_shape = jax.ShapeDtypeStruct((depth, n_batch, N_HEADS, seq, HEAD_DIM), F32)
    in_specs = [blk_spec(0), blk_spec(N_HEADS), blk_spec(2 * N_HEADS)]
    args = [proj, proj, proj]
    body, aliases = _moba_prompt_kernel, {}
    if kv_prev is None:
        kv_out = pl.BlockSpec((depth, None, None, seq, HEAD_DIM), lambda b, h: (0, b, h, 0, 0))
    else:
        kv_out = pl.BlockSpec((None, None, None, seq, HEAD_DIM), lambda b, h: (layer, b, h, 0, 0))
        in_specs += [pl.BlockSpec(memory_space=pl.ANY)] * 2
        args += list(kv_prev)
        body, aliases = _moba_prompt_kernel_aliased, {3: 1, 4: 2}
    return pl.pallas_call(
        functools.partial(body, nb=nb),
        grid=(n_batch, N_HEADS),
        in_specs=in_specs,
        out_specs=[pl.BlockSpec((seq, HEAD_DIM), lambda b, h: (b, h)), kv_out, kv_out],
        out_shape=[jax.ShapeDtypeStruct((n_batch * seq, ATTN_W), BF16), kv_shape, kv_shape],
        scratch_shapes=[pltpu.VMEM((MOBA_BLOCK, seq), F32)],
        input_output_aliases=aliases,
        compiler_params=_params(),
        name="moba_prompt",
    )(*args)


def _moba_sample_kernel(pt_ref, q_ref, kn_ref, vn_ref, *refs, n_pages, n_new):
    del pt_ref
    k_refs = refs[:n_pages]
    v_refs = refs[n_pages:2 * n_pages]
    o_ref = refs[2 * n_pages]
    st_ref, vcat_ref, km_ref = refs[2 * n_pages + 1:]
    ps = PAGE_SIZE
    ppb = MOBA_BLOCK // ps
    nbp = n_pages // ppb
    k_sel = min(MOBA_TOPK, nbp)
    scale = HEAD_DIM ** -0.5

    q8 = q_ref[0]
    c_i = lax.broadcasted_iota(jnp.int32, (LANE, SUBLANE), 0)
    t_i = lax.broadcasted_iota(jnp.int32, (LANE, SUBLANE), 1)
    rep = ((c_i % SUBLANE == t_i) & (c_i < N_HEADS * SUBLANE)).astype(F32)
    qrep = jnp.dot(rep, q8, precision=_HI, preferred_element_type=F32)
    cc = lax.broadcasted_iota(jnp.int32, (LANE, ATTN_W), 0)
    ll = lax.broadcasted_iota(jnp.int32, (LANE, ATTN_W), 1)
    qrep = jnp.where(ll // HEAD_DIM == cc // SUBLANE, qrep, 0.0)
    qs_b = (qrep * scale).astype(BF16)

    acc = None
    for p in range(n_pages):
        kcat = jnp.concatenate([k_refs[p][h] for h in range(N_HEADS)], axis=1)
        vcat = jnp.concatenate([v_refs[p][h] for h in range(N_HEADS)], axis=1)
        st_ref[p * ps:(p + 1) * ps, :] = lax.dot_general(kcat.astype(BF16), qs_b, _NT,
                                                         preferred_element_type=F32)
        vcat_ref[p * ps:(p + 1) * ps, :] = vcat.astype(BF16)
        colsum = jnp.sum(kcat, axis=0, keepdims=True)
        acc = colsum if p % ppb == 0 else acc + colsum
        if p % ppb == ppb - 1:
            j = p // ppb
            km_ref[j:j + 1, :] = acc * (1.0 / MOBA_BLOCK)

    gate = lax.dot_general(km_ref[...], qrep, _NT, precision=_HI, preferred_element_type=F32)
    jidx = lax.broadcasted_iota(jnp.int32, (nbp, LANE), 0)
    sel = []
    for j in range(nbp):
        gj = gate[j:j + 1, :]
        beats = (gate > gj) | ((gate == gj) & (jidx < j))
        sel.append(jnp.sum(beats.astype(F32), axis=0, keepdims=True) < k_sel)

    kn8 = kn_ref[0]
    vn8 = vn_ref[0]
    sn = lax.dot_general(kn8.astype(BF16), qs_b, _NT, preferred_element_type=F32)
    s_i = lax.broadcasted_iota(jnp.int32, (SUBLANE, LANE), 0)
    t_c = lax.broadcasted_iota(jnp.int32, (SUBLANE, LANE), 1) % SUBLANE
    sn = jnp.where((s_i <= t_c) & (s_i < n_new), sn, NEG)
    m = jnp.max(sn, axis=0, keepdims=True)
    for j in range(nbp):
        mj = jnp.max(st_ref[j * MOBA_BLOCK:(j + 1) * MOBA_BLOCK, :], axis=0, keepdims=True)
        m = jnp.where(sel[j], jnp.maximum(m, mj), m)
    pn = jnp.exp(sn - m)
    l = jnp.sum(pn, axis=0, keepdims=True)
    for j in range(nbp):
        rows = slice(j * MOBA_BLOCK, (j + 1) * MOBA_BLOCK)
        pj = jnp.where(sel[j], jnp.exp(st_ref[rows, :] - m), 0.0)
        l = l + jnp.sum(pj, axis=0, keepdims=True)
        st_ref[rows, :] = pj
    inv = 1.0 / l
    pt = (st_ref[...] * inv).astype(BF16)
    o_full = lax.dot_general(pt, vcat_ref[...], _TN, preferred_element_type=F32)
    o_full = o_full + lax.dot_general((pn * inv).astype(BF16), vn8.astype(BF16), _TN,
                                      preferred_element_type=F32)
    o_ref[0] = jnp.concatenate(
        [o_full[h * SUBLANE:(h + 1) * SUBLANE, h * HEAD_DIM:(h + 1) * HEAD_DIM] for h in range(N_HEADS)], axis=1)


def _moba_sample(q8, kn8, vn8, cache_k, cache_v, page_table, layer, n_new):
    assert n_new <= SUBLANE
    n_seq, n_pages = page_table.shape
    assert (n_pages * PAGE_SIZE) % MOBA_BLOCK == 0
    tok_spec = pl.BlockSpec((1, SUBLANE, ATTN_W), lambda s, pt: (s, 0, 0))

    def page_spec(p):
        return pl.BlockSpec((None, None, N_HEADS, PAGE_SIZE, HEAD_DIM),
                            lambda s, pt: (layer, pt[s, p], 0, 0, 0))

    past = n_pages * PAGE_SIZE
    grid_spec = pltpu.PrefetchScalarGridSpec(
        num_scalar_prefetch=1,
        grid=(n_seq,),
        in_specs=[tok_spec, tok_spec, tok_spec] + [page_spec(p) for p in range(n_pages)] * 2,
        out_specs=tok_spec,
        scratch_shapes=[pltpu.VMEM((past, LANE), F32), pltpu.VMEM((past, ATTN_W), BF16),
                        pltpu.VMEM((past // MOBA_BLOCK, ATTN_W), F32)],
    )
    return pl.pallas_call(
        functools.partial(_moba_sample_kernel, n_pages=n_pages, n_new=n_new),
        grid_spec=grid_spec,
        out_shape=jax.ShapeDtypeStruct((n_seq, SUBLANE, ATTN_W), F32),
        compiler_params=_params(),
        name="moba_sample",
    )(page_table, q8, kn8, vn8, *([cache_k] * n_pages), *([cache_v] * n_pages))


def _pool_prompt_kernel(halo_ref, u_ref, d_ref, *, tile, halo):
    t = pl.program_id(1)
    u = u_ref[...]
    prev = jnp.where(t > 0, halo_ref[...], 0.0)
    ucat = jnp.concatenate([prev, u], axis=0)
    hi = ucat.astype(BF16)
    lo = (ucat - hi.astype(F32)).astype(BF16)
    r = lax.broadcasted_iota(jnp.int32, (tile, halo + tile), 0) + halo
    c = lax.broadcasted_iota(jnp.int32, (tile, halo + tile), 1)
    pos = t * tile + lax.broadcasted_iota(jnp.int32, (tile, 1), 0)
    for g, w in enumerate(POOL_WINDOWS):
        band = ((c <= r) & (c > r - w)).astype(BF16)
        sl = slice(g * POOL_GW, (g + 1) * POOL_GW)
        acc = (jnp.dot(band, hi[:, sl], preferred_element_type=F32)
               + jnp.dot(band, lo[:, sl], preferred_element_type=F32))
        cnt = jnp.minimum(pos + 1, w).astype(F32)
        d_ref[:, sl] = (acc / cnt - u[:, sl]).astype(d_ref.dtype)


def _pool_prompt(proj, n_batch, seq, u_col, *, tile=256, halo=128):
    assert halo >= POOL_HIST and tile % halo == 0 and seq % tile == 0
    tiles = seq // tile
    per = tile // halo
    return pl.pallas_call(
        functools.partial(_pool_prompt_kernel, tile=tile, halo=halo),
        grid=(n_batch, tiles),
        in_specs=[pl.BlockSpec((halo, POOL_W), lambda b, t: (jnp.maximum((b * tiles + t) * per - 1, 0), u_col)),
                  pl.BlockSpec((tile, POOL_W), lambda b, t: (b * tiles + t, u_col))],
        out_specs=pl.BlockSpec((tile, POOL_W), lambda b, t: (b * tiles + t, 0)),
        out_shape=jax.ShapeDtypeStruct((n_batch * seq, POOL_W), BF16),
        compiler_params=_params(),
        name="pool_prompt",
    )(proj, proj)


def _pool_sample_kernel(h_ref, u_ref, d_ref, *, n_hist, n_new, pos0):
    for g, w in enumerate(POOL_WINDOWS):
        sl = slice(g * POOL_GW, (g + 1) * POOL_GW)
        rows = [h_ref[j, :, sl] for j in range(n_hist)] + [u_ref[t, :, sl] for t in range(n_new)]
        for t in range(n_new):
            ti = n_hist + t
            first = max(ti + 1 - w, 0)
            acc = rows[first]
            for j in range(first + 1, ti + 1):
                acc = acc + rows[j]
            cnt = float(min(pos0 + t + 1, w))
            d_ref[t, :, sl] = (acc / cnt - rows[ti]).astype(d_ref.dtype)


def _pool_sample(hist_t, u_t, pos0, *, chunk=32):
    n_hist, n_seq, _ = hist_t.shape
    n_new = u_t.shape[0]
    return pl.pallas_call(
        functools.partial(_pool_sample_kernel, n_hist=n_hist, n_new=n_new, pos0=pos0),
        grid=(n_seq // chunk,),
        in_specs=[pl.BlockSpec((n_hist, chunk, POOL_W), lambda i: (0, i, 0)),
                  pl.BlockSpec((n_new, chunk, POOL_W), lambda i: (0, i, 0))],
        out_specs=pl.BlockSpec((n_new, chunk, POOL_W), lambda i: (0, i, 0)),
        out_shape=jax.ShapeDtypeStruct((n_new, n_seq, POOL_W), BF16),
        compiler_params=_params(),
        name="pool_sample",
    )(hist_t, u_t)


def _merge_kernel(ap_ref, as_ref, dp_ref, ds_ref, ga_ref, gb_ref, wao_ref, wgrp_ref, wpo_ref, sc_ref, m_ref, *,
                  prompt_tiles):
    is_prompt = pl.program_id(0) < prompt_tiles
    attn = jnp.where(is_prompt, ap_ref[...], as_ref[...])
    dd = jnp.where(is_prompt, dp_ref[...], ds_ref[...])
    a = jnp.dot(attn, wao_ref[...], preferred_element_type=F32)
    ys = [jnp.dot(dd[:, g * POOL_GW:(g + 1) * POOL_GW], wgrp_ref[g], preferred_element_type=F32)
          for g in range(len(POOL_WINDOWS))]
    pooled = (jnp.concatenate(ys, axis=1) * sc_ref[...]).astype(BF16)
    b = jnp.dot(pooled, wpo_ref[...], preferred_element_type=F32)
    m = jax.nn.sigmoid(ga_ref[...]) * a + jax.nn.sigmoid(gb_ref[...]) * b
    m_ref[...] = m.astype(m_ref.dtype)


def _merge(attn_p, attn_s, d_p, d_s, proj, ga_col, w_ao, w_grp, w_po, scale, layer, *, tm=256):
    n = proj.shape[0]
    dm = w_ao.shape[2]
    assert attn_p.shape[0] % tm == 0 and attn_s.shape[0] % tm == 0
    pt = attn_p.shape[0] // tm
    return pl.pallas_call(
        functools.partial(_merge_kernel, prompt_tiles=pt),
        grid=(n // tm,),
        in_specs=[*_split_rows(tm, ATTN_W, pt), *_split_rows(tm, POOL_W, pt),
                  pl.BlockSpec((tm, dm), lambda i: (i, ga_col)),
                  pl.BlockSpec((tm, dm), lambda i: (i, ga_col + 1)),
                  _layer_spec(w_ao.shape[1:], layer), _layer_spec(w_grp.shape[1:], layer),
                  _layer_spec(w_po.shape[1:], layer), _layer_spec((1, POOL_W), layer)],
        out_specs=pl.BlockSpec((tm, dm), lambda i: (i, 0)),
        out_shape=jax.ShapeDtypeStruct((n, dm), BF16),
        compiler_params=_params(),
        name="merge",
    )(attn_p, attn_s, d_p, d_s, proj, proj, w_ao, w_grp, w_po, scale)


def _first_lane_of_max(vals, lane):
    top = jnp.max(vals, axis=-1, keepdims=True)
    idx = jnp.min(jnp.where(vals == top, lane, LANE), axis=-1, keepdims=True)
    return top, idx


def _one_row(ref, row):
    return ref.at[pl.ds(row, 1)]


def _outproj_router_kernel(x_ref, m_ref, wo_ref, gf_ref, wr_ref, br_ref, x1_ref, h_ref, route_ref, cnt_ref,
                           run_ref, wr3_ref):
    step = pl.program_id(0)

    d = x_ref.shape[1]

    @pl.when(step == 0)
    def _():
        run_ref[...] = jnp.zeros_like(run_ref)
        w = wr_ref[...]
        w_hi = w.astype(BF16)
        wr3_ref[0:d, :] = w_hi
        wr3_ref[d:2 * d, :] = w_hi
        wr3_ref[2 * d:3 * d, :] = (w - w_hi.astype(F32)).astype(BF16)

    x1 = x_ref[...] + jnp.dot(m_ref[...], wo_ref[...], preferred_element_type=F32)
    x1_ref[...] = x1
    h = _rmsnorm(x1, gf_ref[...])
    h_ref[...] = h
    h_hi = h.astype(BF16)
    h_lo = (h - h_hi.astype(F32)).astype(BF16)
    logits = jnp.dot(jnp.concatenate([h_hi, h_lo, h_hi], axis=1), wr3_ref[...],
                     preferred_element_type=F32) + br_ref[...]
    lane = lax.broadcasted_iota(jnp.int32, logits.shape, 1)
    is_g = (lane >= N_EXPERTS) & (lane < N_EXPERTS + N_EXPERT_GROUPS)
    gmax, gidx = _first_lane_of_max(jnp.where(is_g, logits, -jnp.inf), lane)
    p_g = 1.0 / jnp.sum(jnp.where(is_g, jnp.exp(logits - gmax), 0.0), axis=-1, keepdims=True)
    in_grp = (lane // EXPERTS_PER_GROUP == gidx - N_EXPERTS) & (lane < N_EXPERTS)
    e1 = jnp.where(in_grp, logits, -jnp.inf)
    t1, i1 = _first_lane_of_max(e1, lane)
    t2, i2 = _first_lane_of_max(jnp.where(lane == i1, -jnp.inf, e1), lane)
    z = jnp.exp(t2 - t1)
    w1 = p_g / (1.0 + z)
    w2 = p_g * (z / (1.0 + z))
    oh1 = lane == i1
    oh2 = lane == i2
    picks = (oh1 | oh2).astype(F32)
    tm = picks.shape[0]
    earlier = (lax.broadcasted_iota(jnp.int32, (tm, tm), 1)
               < lax.broadcasted_iota(jnp.int32, (tm, tm), 0)).astype(BF16)
    before = jnp.dot(earlier, picks.astype(BF16), preferred_element_type=F32) + run_ref[...]
    r1 = jnp.sum(jnp.where(oh1, before, 0.0), axis=-1, keepdims=True)
    r2 = jnp.sum(jnp.where(oh2, before, 0.0), axis=-1, keepdims=True)
    run_ref[...] = run_ref[...] + jnp.sum(picks, axis=0, keepdims=True)
    cnt_ref[...] = jnp.broadcast_to(run_ref[...], cnt_ref.shape)
    route = jnp.zeros(logits.shape, F32)
    for k, val in enumerate((i1.astype(F32), i2.astype(F32), w1, w2, r1, r2)):
        route = jnp.where(lane == k, val, route)
    route_ref[...] = route


def _outproj_router(x, m, w_o, g_f, w_router, b_router, layer, *, tm=256):
    n, d = x.shape
    row = lambda width: pl.BlockSpec((tm, width), lambda i: (i, 0))
    return pl.pallas_call(
        _outproj_router_kernel,
        grid=(n // tm,),
        in_specs=[row(d), row(d), _layer_spec((d, d), layer), _layer_spec((1, d), layer),
                  _layer_spec((d, LANE), layer), _layer_spec((1, LANE), layer)],
        out_specs=[row(d), row(d), row(LANE), pl.BlockSpec((SUBLANE, LANE), lambda i: (0, 0))],
        out_shape=[jax.ShapeDtypeStruct((n, d), F32), jax.ShapeDtypeStruct((n, d), F32),
                   jax.ShapeDtypeStruct((n, LANE), F32), jax.ShapeDtypeStruct((SUBLANE, LANE), F32)],
        scratch_shapes=[pltpu.VMEM((1, LANE), F32), pltpu.VMEM((3 * d, LANE), BF16)],
        compiler_params=_params(),
        name="outproj_router",
    )(x, m, w_o, g_f, w_router, b_router)


def _dispatch_tables(route, counts_f):
    n = route.shape[0]
    a = n * TOPK_IN_GROUP
    n_blocks = a // MOE_BLOCK
    assert n_blocks * MOE_BLOCK == a
    n_items = n_blocks + N_EXPERTS - 1
    counts = counts_f[0, :N_EXPERTS].astype(jnp.int32)
    end = jnp.cumsum(counts)
    start = end - counts
    eid = route[:, :TOPK_IN_GROUP].astype(jnp.int32)
    rank = route[:, 2 * TOPK_IN_GROUP:3 * TOPK_IN_GROUP].astype(jnp.int32)
    onehot = eid[:, :, None] == jnp.arange(N_EXPERTS, dtype=jnp.int32)[None, None, :]
    dest = (jnp.sum(jnp.where(onehot, start[None, None, :], 0), axis=-1) + rank).reshape(a)

    first_blk = start // MOE_BLOCK
    items_e = jnp.where(counts > 0, (end - 1) // MOE_BLOCK - first_blk + 1, 0)
    item_end = jnp.cumsum(items_e)
    item_off = item_end - items_e
    total = item_end[-1]
    idx = jnp.arange(n_items, dtype=jnp.int32)
    e = jnp.minimum(jnp.sum((idx[:, None] >= item_end[None, :]).astype(jnp.int32), axis=1), N_EXPERTS - 1)
    pick = lambda tbl, ee: jnp.sum(jnp.where(ee[:, None] == jnp.arange(N_EXPERTS)[None, :], tbl[None, :], 0), axis=1)
    blk = pick(first_blk, e) + idx - pick(item_off, e)
    lo = jnp.maximum(pick(start, e), blk * MOE_BLOCK) - blk * MOE_BLOCK
    hi = jnp.minimum(pick(end, e), (blk + 1) * MOE_BLOCK) - blk * MOE_BLOCK
    valid = idx < total
    e_last = jnp.sum(jnp.where(idx == total - 1, e, 0))
    e = jnp.where(valid, e, e_last)
    blk = jnp.where(valid, blk, n_blocks - 1)
    lo = jnp.where(valid, lo, 0)
    hi = jnp.where(valid, hi, 0)
    first = jnp.concatenate([jnp.ones((1,), jnp.int32), (blk[1:] != blk[:-1]).astype(jnp.int32)])
    ids = jnp.arange(N_EXPERTS, dtype=jnp.int32)
    later = (ids[None, :] > ids[:, None]) & (counts[None, :] > 0)
    nxt_e = jnp.min(jnp.where(later, ids[None, :], N_EXPERTS), axis=1)
    nxt = pick(jnp.where(nxt_e < N_EXPERTS, nxt_e, -1), e)
    i32 = lambda t: t.astype(jnp.int32)
    return i32(dest), i32(blk), i32(e), i32(lo), i32(hi), first, i32(nxt)


def _dispatch_kernel(dest_ref, h_ref, xs_hbm, sem, *, tm):
    base = pl.program_id(0) * tm * TOPK_IN_GROUP

    def start(g, carry):
        r0 = pl.multiple_of(g * SUBLANE, SUBLANE)
        for u in range(SUBLANE):
            for k in range(TOPK_IN_GROUP):
                slot = dest_ref[base + (r0 + u) * TOPK_IN_GROUP + k]
                pltpu.make_async_copy(_one_row(h_ref, r0 + u), _one_row(xs_hbm, slot), sem).start(priority=k)
        return carry
    lax.fori_loop(0, tm // SUBLANE, start, 0)

    def wait(r, carry):
        for k in range(TOPK_IN_GROUP):
            pltpu.make_async_copy(_one_row(h_ref, 0), _one_row(xs_hbm, 0), sem).wait()
        return carry
    lax.fori_loop(0, tm, wait, 0, unroll=8)


def _dispatch(dest, h, *, tm=256):
    n, d = h.shape
    grid_spec = pltpu.PrefetchScalarGridSpec(
        num_scalar_prefetch=1,
        grid=(n // tm,),
        in_specs=[pl.BlockSpec((tm, d), lambda i, dr: (i, 0))],
        out_specs=pl.BlockSpec(memory_space=pl.ANY),
        scratch_shapes=[pltpu.SemaphoreType.DMA(())],
    )
    return pl.pallas_call(
        functools.partial(_dispatch_kernel, tm=tm),
        grid_spec=grid_spec,
        out_shape=jax.ShapeDtypeStruct((n * TOPK_IN_GROUP, d), F32),
        compiler_params=_params(),
        name="moe_dispatch",
    )(dest, h)


def _moe_ffn_kernel(blk_ref, e_ref, lo_ref, hi_ref, first_ref, nxt_ref, x_ref, wg_hbm, wu_hbm, wd_hbm, y_ref,
                    stage_g, stage_u, stage_d, wg_b, wu_b, wd_b, sems, cur_ref, *, layer):
    del blk_ref
    j = pl.program_id(0)

    def weight_copies(expert, s):
        return (pltpu.make_async_copy(wg_hbm.at[layer, expert], stage_g.at[s], sems.at[s]),
                pltpu.make_async_copy(wu_hbm.at[layer, expert], stage_u.at[s], sems.at[s]),
                pltpu.make_async_copy(wd_hbm.at[layer, expert], stage_d.at[s], sems.at[s]))

    @pl.when(j == 0)
    def _():
        cur_ref[0] = 1
        for c in weight_copies(e_ref[0], 0):
            c.start(priority=1)

    @pl.when((j == 0) | (e_ref[j] != e_ref[jnp.maximum(j - 1, 0)]))
    def _():
        s = 1 - cur_ref[0]
        cur_ref[0] = s

        @pl.when(nxt_ref[j] >= 0)
        def _():
            for c in weight_copies(nxt_ref[j], 1 - s):
                c.start(priority=1)

        for c in weight_copies(e_ref[j], s):
            c.wait()
        wg_b[...] = stage_g[s].astype(BF16)
        wu_b[...] = stage_u[s].astype(BF16)
        wd_b[...] = stage_d[s].astype(BF16)

    @pl.when(hi_ref[j] > lo_ref[j])
    def _():
        x = x_ref[...].astype(BF16)
        gate = jnp.dot(x, wg_b[...], preferred_element_type=F32)
        up = jnp.dot(x, wu_b[...], preferred_element_type=F32)
        act = (jax.nn.silu(gate) * up).astype(BF16)
        y = jnp.dot(act, wd_b[...], preferred_element_type=F32)
        row = lax.broadcasted_iota(jnp.int32, (MOE_BLOCK, 1), 0)
        y = jnp.where((row >= lo_ref[j]) & (row < hi_ref[j]), y, 0.0)

        @pl.when(first_ref[j] == 1)
        def _():
            y_ref[...] = y

        @pl.when(first_ref[j] == 0)
        def _():
            y_ref[...] = y_ref[...] + y


def _moe_ffn(xs, blk, e, lo, hi, first, nxt, w_gate, w_up, w_down, layer):
    a, d = xs.shape
    de = w_gate.shape[3]
    n_items = blk.shape[0]
    slab = pl.BlockSpec((MOE_BLOCK, d), lambda j, blk, *_: (blk[j], 0))
    hbm = pl.BlockSpec(memory_space=pl.ANY)
    grid_spec = pltpu.PrefetchScalarGridSpec(
        num_scalar_prefetch=6,
        grid=(n_items,),
        in_specs=[slab, hbm, hbm, hbm],
        out_specs=slab,
        scratch_shapes=[pltpu.VMEM((2, d, de), F32), pltpu.VMEM((2, d, de), F32), pltpu.VMEM((2, de, d), F32),
                        pltpu.VMEM((d, de), BF16), pltpu.VMEM((d, de), BF16), pltpu.VMEM((de, d), BF16),
                        pltpu.SemaphoreType.DMA((2,)), pltpu.SMEM((1,), jnp.int32)],
    )
    return pl.pallas_call(
        functools.partial(_moe_ffn_kernel, layer=layer),
        grid_spec=grid_spec,
        out_shape=jax.ShapeDtypeStruct((a, d), F32),
        compiler_params=_params(),
        name="moe_ffn",
    )(blk, e, lo, hi, first, nxt, xs, w_gate, w_up, w_down)


def _combine_ple_kernel(dest_ref, x_ref, route_ref, yb_hbm, pp_ref, ps_ref, gp_ref, wplg_ref, wple_ref, gfin_ref,
                        *rest, tm, prompt_steps, final):
    outs, (ybuf_a, ybuf_b, sems) = rest[:-3], rest[-3:]
    bufs = (ybuf_a, ybuf_b)
    i = pl.program_id(0)
    nt = pl.num_programs(0)

    def row_copy(tile, r, k, s):
        src = dest_ref[(tile * tm + r) * TOPK_IN_GROUP + k]
        return pltpu.make_async_copy(_one_row(yb_hbm, src), _one_row(bufs[s].at[k], r), sems.at[s])

    def start_gather(tile, s, looped=False):
        def rows8(g, carry):
            for u in range(SUBLANE):
                for k in range(TOPK_IN_GROUP):
                    row_copy(tile, g * SUBLANE + u, k, s).start(priority=k)
            return carry
        if looped:
            lax.fori_loop(0, tm // SUBLANE, rows8, 0)
        else:
            for g in range(tm // SUBLANE):
                rows8(g, 0)

    def wait_gather(s):
        def body(r, carry):
            for k in range(TOPK_IN_GROUP):
                row_copy(0, 0, k, s).wait()
            return carry
        lax.fori_loop(0, tm, body, 0, unroll=8)

    def compute(half, s):
        rows = pl.ds(half * tm, tm)
        route = route_ref[rows, :]
        moe = sum(bufs[s][k] * route[:, TOPK_IN_GROUP + k:TOPK_IN_GROUP + k + 1] for k in range(TOPK_IN_GROUP))
        x2 = x_ref[rows, :] + moe
        hp = _rmsnorm(x2, gp_ref[...]).astype(BF16)
        gate = jax.nn.sigmoid(jnp.dot(hp, wplg_ref[...], preferred_element_type=F32))
        p = jnp.where(i < prompt_steps, pp_ref[rows, :], ps_ref[rows, :])
        pe = jnp.dot(p.astype(BF16), wple_ref[...], preferred_element_type=F32)
        x3 = x2 + gate * pe
        if not final:
            outs[0][rows, :] = x3
        else:
            y = _rmsnorm(x3, gfin_ref[...])

            @pl.when(i < prompt_steps)
            def _():
                outs[0][rows, :] = y

            @pl.when(i >= prompt_steps)
            def _():
                outs[1][rows, :] = y

    @pl.when(i == 0)
    def _():
        start_gather(0, 0, looped=True)

    wait_gather(0)
    start_gather(2 * i + 1, 1)
    compute(0, 0)
    wait_gather(1)
    start_gather(jnp.minimum(2 * i + 2, 2 * nt - 2), 0)
    compute(1, 1)

    @pl.when(i == nt - 1)
    def _():
        wait_gather(0)


def _split_rows(tm, width, pt, lead=()):
    none = (None,) * len(lead)
    return (pl.BlockSpec(none + (tm, width), lambda i, *_: lead + (jnp.minimum(i, pt - 1), 0)),
            pl.BlockSpec(none + (tm, width), lambda i, *_: lead + (jnp.maximum(i - pt, 0), 0)))


def _combine_ple(dest, x1, route, yb, p_prompt, p_sample, g_p, w_plg, w_ple, g_final, layer, *, final, tm=128):
    n, d = x1.shape
    n_prompt, pd = p_prompt.shape[1:]
    step = 2 * tm
    assert n_prompt % step == 0 and (n - n_prompt) % step == 0
    ps_ = n_prompt // step
    row = lambda width: pl.BlockSpec((step, width), lambda i, dr: (i, 0))
    if final:
        out_specs = list(_split_rows(step, d, ps_))
        out_shape = [jax.ShapeDtypeStruct((n_prompt, d), F32), jax.ShapeDtypeStruct((n - n_prompt, d), F32)]
    else:
        out_specs = [row(d)]
        out_shape = [jax.ShapeDtypeStruct((n, d), F32)]
    gather_buf = pltpu.VMEM((TOPK_IN_GROUP, tm, d), F32)
    grid_spec = pltpu.PrefetchScalarGridSpec(
        num_scalar_prefetch=1,
        grid=(n // step,),
        in_specs=[row(d), row(LANE), pl.BlockSpec(memory_space=pl.ANY), *_split_rows(step, pd, ps_, (layer,)),
                  _layer_spec((1, d), layer), _layer_spec((d, d), layer), _layer_spec((pd, d), layer),
                  pl.BlockSpec((1, d), lambda i, dr: (0, 0))],
        out_specs=out_specs,
        scratch_shapes=[gather_buf, gather_buf, pltpu.SemaphoreType.DMA((2,))],
    )
    return pl.pallas_call(
        functools.partial(_combine_ple_kernel, tm=tm, prompt_steps=ps_, final=final),
        grid_spec=grid_spec,
        out_shape=out_shape,
        compiler_params=_params(),
        name="combine_ple",
    )(dest, x1, route, yb, p_prompt, p_sample, g_p, w_plg, w_ple, g_final.reshape(1, d))


def kernel(x_prompt, x_sample, cache_k, cache_v, state_pool, page_table, p_prompt, p_sample, w_in, w_attn_out, w_pool_out, w_out, w_pool_group, pool_scale, g_mix, g_ffn, g_ple, g_final, w_router_group, b_router_group, w_router_expert, b_router_expert, w_exp_gate, w_exp_up, w_exp_down, w_ple, w_ple_gate):
    nb_, seq, d = x_prompt.shape
    n_seq, n_new, _ = x_sample.shape
    depth = w_in.shape[0]
    n_p = nb_ * seq
    n_s = n_seq * n_new
    past_len = page_table.shape[1] * PAGE_SIZE
    u_col = 3 * ATTN_W // POOL_W
    ga_col = (3 * ATTN_W + POOL_W) // d
    assert 3 * ATTN_W == u_col * POOL_W and 3 * ATTN_W + POOL_W == ga_col * d

    x = jnp.concatenate([x_prompt.reshape(n_p, d), x_sample.reshape(n_s, d)], axis=0)
    outs = {name: [] for name in ("pool_p", "ks", "vs", "pool_s")}
    pad_new = lambda t: jnp.pad(t.reshape(n_seq, n_new, ATTN_W), ((0, 0), (0, SUBLANE - n_new), (0, 0)))
    heads_s = lambda t: t.reshape(n_seq, n_new, N_HEADS, HEAD_DIM).transpose(0, 2, 1, 3)

    per_layer_row = lambda g: g.reshape(depth, 1, g.shape[-1])
    w_in_b, w_ao_b, w_po_b, w_o_b, w_grp_b, w_plg_b, w_ple_b = (
        w.astype(BF16) for w in (w_in, w_attn_out, w_pool_out, w_out, w_pool_group, w_ple_gate, w_ple))
    lane_pad = LANE - N_EXPERTS - N_EXPERT_GROUPS
    w_router = jnp.pad(jnp.concatenate([w_router_expert, w_router_group], axis=2), ((0, 0), (0, 0), (0, lane_pad)))
    b_router = jnp.pad(jnp.concatenate([b_router_expert, b_router_group], axis=1),
                       ((0, 0), (0, lane_pad))).reshape(depth, 1, LANE)
    pp, ps = p_prompt.reshape(depth, n_p, -1), p_sample.reshape(depth, n_s, -1)
    kv_p = None

    for i in range(depth):
        proj = _inproj(x, per_layer_row(g_mix), w_in_b, i)
        attn_p, k_all, v_all = _moba_prompt(proj, nb_, seq, i, depth, kv_p)
        kv_p = (k_all, v_all)
        q_s, k_s, v_s = (proj[n_p:, c * ATTN_W:(c + 1) * ATTN_W] for c in range(3))
        attn_s = _moba_sample(pad_new(q_s), pad_new(k_s), pad_new(v_s), cache_k, cache_v, page_table, i, n_new)
        attn_s = attn_s[:, :n_new].reshape(n_s, ATTN_W).astype(BF16)
        u_p = proj[:n_p, 3 * ATTN_W:3 * ATTN_W + POOL_W].reshape(nb_, seq, POOL_W)
        u_s = proj[n_p:, 3 * ATTN_W:3 * ATTN_W + POOL_W].reshape(n_seq, n_new, POOL_W)
        d_p = _pool_prompt(proj, nb_, seq, u_col)
        d_s = _pool_sample(state_pool[i].transpose(1, 0, 2), u_s.transpose(1, 0, 2), past_len)
        d_s = d_s.transpose(1, 0, 2).reshape(n_s, POOL_W)
        m = _merge(attn_p, attn_s, d_p, d_s, proj, ga_col, w_ao_b, w_grp_b, w_po_b, per_layer_row(pool_scale), i)
        x1, h, route, counts = _outproj_router(x, m, w_o_b, per_layer_row(g_ffn), w_router, b_router, i)
        dest, blk, e, lo, hi, first, nxt = _dispatch_tables(route, counts)
        xs = _dispatch(dest, h)
        yb = _moe_ffn(xs, blk, e, lo, hi, first, nxt, w_exp_gate, w_exp_up, w_exp_down, i)
        res = _combine_ple(dest, x1, route, yb, pp, ps, per_layer_row(g_ple), w_plg_b, w_ple_b, g_final, i,
                           final=(i == depth - 1))
        x = res[0]

        outs["pool_p"].append(u_p[:, seq - POOL_HIST:])
        outs["ks"].append(heads_s(k_s))
        outs["vs"].append(heads_s(v_s))
        outs["pool_s"].append(jnp.concatenate([state_pool[i], u_s], axis=1)[:, n_new:])

    y_prompt = res[0].reshape(nb_, seq, d)
    y_sample = res[1].reshape(n_seq, n_new, d)
    stack = lambda name: jnp.stack(outs[name])
    return (y_prompt, y_sample, kv_p[0], kv_p[1], stack("pool_p"),
            stack("ks"), stack("vs"), stack("pool_s"))
```

```python
import functools

import jax
import jax.numpy as jnp
from jax import lax
from jax.experimental import pallas as pl
from jax.experimental.pallas import tpu as pltpu

F32 = jnp.float32
BF16 = jnp.bfloat16

LANE = 128
SUBLANE = 8
V7X_VMEM_LIMIT_BYTES = 56 * 1024 * 1024

N_HEADS = 8
HEAD_DIM = 128
ATTN_W = N_HEADS * HEAD_DIM
MOBA_BLOCK = 256
MOBA_TOPK = 3
PAGE_SIZE = 128
POOL_WINDOWS = (2, 4, 8, 16)
POOL_GW = 256
POOL_W = len(POOL_WINDOWS) * POOL_GW
POOL_HIST = max(POOL_WINDOWS) - 1
N_EXPERT_GROUPS = 4
EXPERTS_PER_GROUP = 8
N_EXPERTS = N_EXPERT_GROUPS * EXPERTS_PER_GROUP
TOPK_IN_GROUP = 2
MOE_BLOCK = 128
RMS_EPS = 1e-6
NEG = -1e30

_NT = (((1,), (1,)), ((), ()))
_TN = (((0,), (0,)), ((), ()))
_HI = lax.Precision.HIGHEST


def _params():
    return pltpu.CompilerParams(vmem_limit_bytes=V7X_VMEM_LIMIT_BYTES)


def _rmsnorm(x, g):
    return x * lax.rsqrt(jnp.mean(x * x, axis=-1, keepdims=True) + RMS_EPS) * g


def _inproj_kernel(x_ref, g_ref, w_ref, o_ref, xn_ref):
    @pl.when(pl.program_id(1) == 0)
    def _():
        xn_ref[...] = _rmsnorm(x_ref[...], g_ref[...]).astype(BF16)

    o_ref[...] = jnp.dot(xn_ref[...], w_ref[...], preferred_element_type=F32)


def _layer_spec(shape, layer):
    return pl.BlockSpec((None,) + tuple(shape), lambda *_: (layer,) + (0,) * len(shape))


def _row_tile(n, target):
    best = SUBLANE
    for t in range(SUBLANE, target + 1, SUBLANE):
        if n % t == 0:
            best = t
    return best


def _inproj(x, g_all, w_all_bf16, layer, *, tn=1024):
    n, d = x.shape
    pw = w_all_bf16.shape[2]
    tm = _row_tile(n, 1152)
    return pl.pallas_call(
        _inproj_kernel,
        grid=(n // tm, pw // tn),
        in_specs=[pl.BlockSpec((tm, d), lambda i, j: (i, 0)),
                  _layer_spec((1, d), layer),
                  pl.BlockSpec((None, d, tn), lambda i, j: (layer, 0, j))],
        out_specs=pl.BlockSpec((tm, tn), lambda i, j: (i, j)),
        out_shape=jax.ShapeDtypeStruct((n, pw), F32),
        scratch_shapes=[pltpu.VMEM((tm, d), BF16)],
        compiler_params=_params(),
        name="inproj",
    )(x, g_all, w_all_bf16)


def _moba_prompt_kernel(q_ref, k_ref, v_ref, o_ref, ko_ref, vo_ref, s_ref, *, nb):
    blk = MOBA_BLOCK
    scale = HEAD_DIM ** -0.5
    k_sel = min(MOBA_TOPK, nb - 1)
    k = k_ref[...]
    v = v_ref[...]
    ko_ref[...] = jnp.broadcast_to(k, ko_ref.shape)
    vo_ref[...] = jnp.broadcast_to(v, vo_ref.shape)
    kb = k.astype(BF16)
    vb = v.astype(BF16)
    km = jnp.concatenate([jnp.mean(k[j * blk:(j + 1) * blk], axis=0, keepdims=True) for j in range(nb)]
                         + [jnp.zeros((LANE - nb, HEAD_DIM), F32)], axis=0)
    km_hi = km.astype(BF16)
    km3 = jnp.concatenate([km_hi, km_hi, (km - km_hi.astype(F32)).astype(BF16)], axis=1)
    row = lax.broadcasted_iota(jnp.int32, (blk, blk), 0)
    col = lax.broadcasted_iota(jnp.int32, (blk, blk), 1)
    causal = col <= row
    for n in range(nb):
        q = q_ref[n * blk:(n + 1) * blk, :]
        qs = (q * scale).astype(BF16)
        sel = None
        if n > k_sel:
            q_hi = q.astype(BF16)
            q3 = jnp.concatenate([q_hi, (q - q_hi.astype(F32)).astype(BF16), q_hi], axis=1)
            gates = lax.dot_general(q3, km3, _NT, preferred_element_type=F32)
            g = [jnp.broadcast_to(gates[:, j:j + 1], (blk, LANE)) for j in range(n)]
            rank = [jnp.zeros((blk, LANE), F32) for _ in range(n)]
            for j in range(n):
                for jp in range(j):
                    first_wins = (g[jp] >= g[j]).astype(F32)
                    rank[j] = rank[j] + first_wins
                    rank[jp] = rank[jp] + (1.0 - first_wins)
            sel = [r < k_sel for r in rank]
        for j in range(n + 1):
            s = lax.dot_general(qs, kb[j * blk:(j + 1) * blk], _NT, preferred_element_type=F32)
            if j == n:
                s_ref[:, j * blk:(j + 1) * blk] = jnp.where(causal, s, NEG)
            elif sel is None:
                s_ref[:, j * blk:(j + 1) * blk] = s
            else:
                for half in range(blk // LANE):
                    lo = half * LANE
                    s_ref[:, j * blk + lo:j * blk + lo + LANE] = jnp.where(sel[j], s[:, lo:lo + LANE], NEG)
        nk = (n + 1) * blk
        s_all = s_ref[:, :nk]
        m = jnp.max(s_all, axis=-1, keepdims=True)
        p = jnp.exp(s_all - m)
        l = jnp.sum(p, axis=-1, keepdims=True)
        o = jnp.dot(p.astype(BF16), vb[:nk], preferred_element_type=F32) / l
        o_ref[n * blk:(n + 1) * blk, :] = o.astype(o_ref.dtype)


def _moba_prompt_kernel_aliased(q_ref, k_ref, v_ref, kprev_ref, vprev_ref, *rest, nb):
    del kprev_ref, vprev_ref
    _moba_prompt_kernel(q_ref, k_ref, v_ref, *rest, nb=nb)


def _moba_prompt(proj, n_batch, seq, layer, depth, kv_prev):
    nb = seq // MOBA_BLOCK
    assert nb * MOBA_BLOCK == seq
    blk_spec = lambda off: pl.BlockSpec((seq, HEAD_DIM), lambda b, h: (b, off + h))
    kv_shape = jax.ShapeDtypeStruct((depth, n_batch, N_HEADS, seq, HEAD_DIM), F32)
    in_specs = [blk_spec(0), blk_spec(N_HEADS), blk_spec(2 * N_HEADS)]
    args = [proj, proj, proj]
    body, aliases = _moba_prompt_kernel, {}
    if kv_prev is None:
        kv_out = pl.BlockSpec((depth, None, None, seq, HEAD_DIM), lambda b, h: (0, b, h, 0, 0))
    else:
        kv_out = pl.BlockSpec((None, None, None, seq, HEAD_DIM), lambda b, h: (layer, b, h, 0, 0))
        in_specs += [pl.BlockSpec(memory_space=pl.ANY)] * 2
        args += list(kv_prev)
        body, aliases = _moba_prompt_kernel_aliased, {3: 1, 4: 2}
    return pl.pallas_call(
        functools.partial(body, nb=nb),
        grid=(n_batch, N_HEADS),
        in_specs=in_specs,
        out_specs=[pl.BlockSpec((seq, HEAD_DIM), lambda b, h: (b, h)), kv_out, kv_out],
        out_shape=[jax.ShapeDtypeStruct((n_batch * seq, ATTN_W), BF16), kv_shape, kv_shape],
        scratch_shapes=[pltpu.VMEM((MOBA_BLOCK, seq), F32)],
        input_output_aliases=aliases,
        compiler_params=_params(),
        name="moba_prompt",
    )(*args)


def _moba_sample_kernel(pt_ref, q_ref, kn_ref, vn_ref, *refs, n_pages, n_new):
    del pt_ref
    k_refs = refs[:n_pages]
    v_refs = refs[n_pages:2 * n_pages]
    o_ref = refs[2 * n_pages]
    st_ref, vcat_ref, km_ref = refs[2 * n_pages + 1:]
    ps = PAGE_SIZE
    ppb = MOBA_BLOCK // ps
    nbp = n_pages // ppb
    k_sel = min(MOBA_TOPK, nbp)
    scale = HEAD_DIM ** -0.5

    q8 = q_ref[0]
    qrep = jnp.concatenate([q8] * (LANE // SUBLANE), axis=0)
    cc = lax.broadcasted_iota(jnp.int32, (LANE, ATTN_W), 0)
    ll = lax.broadcasted_iota(jnp.int32, (LANE, ATTN_W), 1)
    qrep = jnp.where(ll // HEAD_DIM == cc // SUBLANE, qrep, 0.0)
    qs_b = (qrep * scale).astype(BF16)

    acc = None
    for p in range(n_pages):
        kcat = jnp.concatenate([k_refs[p][h] for h in range(N_HEADS)], axis=1)
        vcat = jnp.concatenate([v_refs[p][h] for h in range(N_HEADS)], axis=1)
        st_ref[p * ps:(p + 1) * ps, :] = lax.dot_general(kcat.astype(BF16), qs_b, _NT,
                                                         preferred_element_type=F32)
        vcat_ref[p * ps:(p + 1) * ps, :] = vcat.astype(BF16)
        colsum = jnp.sum(kcat, axis=0, keepdims=True)
        acc = colsum if p % ppb == 0 else acc + colsum
        if p % ppb == ppb - 1:
            j = p // ppb
            km_ref[j:j + 1, :] = acc * (1.0 / MOBA_BLOCK)

    gate = lax.dot_general(km_ref[...], qrep, _NT, precision=_HI, preferred_element_type=F32)
    jidx = lax.broadcasted_iota(jnp.int32, (nbp, LANE), 0)
    sel = []
    for j in range(nbp):
        gj = gate[j:j + 1, :]
        beats = (gate > gj) | ((gate == gj) & (jidx < j))
        sel.append(jnp.sum(beats.astype(F32), axis=0, keepdims=True) < k_sel)

    kn8 = kn_ref[0]
    vn8 = vn_ref[0]
    sn = lax.dot_general(kn8.astype(BF16), qs_b, _NT, preferred_element_type=F32)
    s_i = lax.broadcasted_iota(jnp.int32, (SUBLANE, LANE), 0)
    t_c = lax.broadcasted_iota(jnp.int32, (SUBLANE, LANE), 1) % SUBLANE
    sn = jnp.where((s_i <= t_c) & (s_i < n_new), sn, NEG)
    m = jnp.max(sn, axis=0, keepdims=True)
    for j in range(nbp):
        mj = jnp.max(st_ref[j * MOBA_BLOCK:(j + 1) * MOBA_BLOCK, :], axis=0, keepdims=True)
        m = jnp.where(sel[j], jnp.maximum(m, mj), m)
    pn = jnp.exp(sn - m)
    l = jnp.sum(pn, axis=0, keepdims=True)
    for j in range(nbp):
        rows = slice(j * MOBA_BLOCK, (j + 1) * MOBA_BLOCK)
        pj = jnp.where(sel[j], jnp.exp(st_ref[rows, :] - m), 0.0)
        l = l + jnp.sum(pj, axis=0, keepdims=True)
        st_ref[rows, :] = pj
    inv = 1.0 / l
    pt = (st_ref[...] * inv).astype(BF16)
    o_full = lax.dot_general(pt, vcat_ref[...], _TN, preferred_element_type=F32)
    o_full = o_full + lax.dot_general((pn * inv).astype(BF16), vn8.astype(BF16), _TN,
                                      preferred_element_type=F32)
    o_ref[0] = jnp.concatenate(
        [o_full[h * SUBLANE:(h + 1) * SUBLANE, h * HEAD_DIM:(h + 1) * HEAD_DIM] for h in range(N_HEADS)], axis=1)


def _moba_sample(q8, kn8, vn8, cache_k, cache_v, page_table, layer, n_new):
    assert n_new <= SUBLANE
    n_seq, n_pages = page_table.shape
    assert (n_pages * PAGE_SIZE) % MOBA_BLOCK == 0
    tok_spec = pl.BlockSpec((1, SUBLANE, ATTN_W), lambda s, pt: (s, 0, 0))

    def page_spec(p):
        return pl.BlockSpec((None, None, N_HEADS, PAGE_SIZE, HEAD_DIM),
                            lambda s, pt: (layer, pt[s, p], 0, 0, 0))

    past = n_pages * PAGE_SIZE
    grid_spec = pltpu.PrefetchScalarGridSpec(
        num_scalar_prefetch=1,
        grid=(n_seq,),
        in_specs=[tok_spec, tok_spec, tok_spec] + [page_spec(p) for p in range(n_pages)] * 2,
        out_specs=tok_spec,
        scratch_shapes=[pltpu.VMEM((past, LANE), F32), pltpu.VMEM((past, ATTN_W), BF16),
                        pltpu.VMEM((past // MOBA_BLOCK, ATTN_W), F32)],
    )
    return pl.pallas_call(
        functools.partial(_moba_sample_kernel, n_pages=n_pages, n_new=n_new),
        grid_spec=grid_spec,
        out_shape=jax.ShapeDtypeStruct((n_seq, SUBLANE, ATTN_W), F32),
        compiler_params=_params(),
        name="moba_sample",
    )(page_table, q8, kn8, vn8, *([cache_k] * n_pages), *([cache_v] * n_pages))


def _pool_prompt_kernel(halo_ref, u_ref, d_ref, *, tile, halo):
    t = pl.program_id(1)
    u = u_ref[...]
    prev = jnp.where(t > 0, halo_ref[...], 0.0)
    ucat = jnp.concatenate([prev, u], axis=0)
    hi = ucat.astype(BF16)
    lo = (ucat - hi.astype(F32)).astype(BF16)
    r = lax.broadcasted_iota(jnp.int32, (tile, halo + tile), 0) + halo
    c = lax.broadcasted_iota(jnp.int32, (tile, halo + tile), 1)
    pos = t * tile + lax.broadcasted_iota(jnp.int32, (tile, 1), 0)
    for g, w in enumerate(POOL_WINDOWS):
        band = ((c <= r) & (c > r - w)).astype(BF16)
        sl = slice(g * POOL_GW, (g + 1) * POOL_GW)
        acc = (jnp.dot(band, hi[:, sl], preferred_element_type=F32)
               + jnp.dot(band, lo[:, sl], preferred_element_type=F32))
        cnt = jnp.minimum(pos + 1, w).astype(F32)
        d_ref[:, sl] = (acc / cnt - u[:, sl]).astype(d_ref.dtype)


def _pool_prompt(proj, n_batch, seq, u_col, *, tile=256, halo=128):
    assert halo >= POOL_HIST and tile % halo == 0 and seq % tile == 0
    tiles = seq // tile
    per = tile // halo
    return pl.pallas_call(
        functools.partial(_pool_prompt_kernel, tile=tile, halo=halo),
        grid=(n_batch, tiles),
        in_specs=[pl.BlockSpec((halo, POOL_W), lambda b, t: (jnp.maximum((b * tiles + t) * per - 1, 0), u_col)),
                  pl.BlockSpec((tile, POOL_W), lambda b, t: (b * tiles + t, u_col))],
        out_specs=pl.BlockSpec((tile, POOL_W), lambda b, t: (b * tiles + t, 0)),
        out_shape=jax.ShapeDtypeStruct((n_batch * seq, POOL_W), BF16),
        compiler_params=_params(),
        name="pool_prompt",
    )(proj, proj)


def _pool_sample_kernel(h_ref, u_ref, d_ref, *, n_hist, n_new, pos0):
    for g, w in enumerate(POOL_WINDOWS):
        sl = slice(g * POOL_GW, (g + 1) * POOL_GW)
        rows = [h_ref[j, :, sl] for j in range(n_hist)] + [u_ref[t, :, sl] for t in range(n_new)]
        for t in range(n_new):
            ti = n_hist + t
            first = max(ti + 1 - w, 0)
            acc = rows[first]
            for j in range(first + 1, ti + 1):
                acc = acc + rows[j]
            cnt = float(min(pos0 + t + 1, w))
            d_ref[t, :, sl] = (acc / cnt - rows[ti]).astype(d_ref.dtype)


def _pool_sample(hist_t, u_t, pos0, *, chunk=32):
    n_hist, n_seq, _ = hist_t.shape
    n_new = u_t.shape[0]
    return pl.pallas_call(
        functools.partial(_pool_sample_kernel, n_hist=n_hist, n_new=n_new, pos0=pos0),
        grid=(n_seq // chunk,),
        in_specs=[pl.BlockSpec((n_hist, chunk, POOL_W), lambda i: (0, i, 0)),
                  pl.BlockSpec((n_new, chunk, POOL_W), lambda i: (0, i, 0))],
        out_specs=pl.BlockSpec((n_new, chunk, POOL_W), lambda i: (0, i, 0)),
        out_shape=jax.ShapeDtypeStruct((n_new, n_seq, POOL_W), BF16),
        compiler_params=_params(),
        name="pool_sample",
    )(hist_t, u_t)


def _merge_kernel(ap_ref, as_ref, dp_ref, ds_ref, ga_ref, gb_ref, wao_ref, wgrp_ref, wpo_ref, sc_ref, m_ref, *,
                  prompt_tiles):
    is_prompt = pl.program_id(0) < prompt_tiles
    attn = jnp.where(is_prompt, ap_ref[...], as_ref[...])
    dd = jnp.where(is_prompt, dp_ref[...], ds_ref[...])
    a = jnp.dot(attn, wao_ref[...], preferred_element_type=F32)
    ys = [jnp.dot(dd[:, g * POOL_GW:(g + 1) * POOL_GW], wgrp_ref[g], preferred_element_type=F32)
          for g in range(len(POOL_WINDOWS))]
    pooled = (jnp.concatenate(ys, axis=1) * sc_ref[...]).astype(BF16)
    b = jnp.dot(pooled, wpo_ref[...], preferred_element_type=F32)
    m = jax.nn.sigmoid(ga_ref[...]) * a + jax.nn.sigmoid(gb_ref[...]) * b
    m_ref[...] = m.astype(m_ref.dtype)


def _merge(attn_p, attn_s, d_p, d_s, proj, ga_col, w_ao, w_grp, w_po, scale, layer, *, tm=256):
    n = proj.shape[0]
    dm = w_ao.shape[2]
    assert attn_p.shape[0] % tm == 0 and attn_s.shape[0] % tm == 0
    pt = attn_p.shape[0] // tm
    return pl.pallas_call(
        functools.partial(_merge_kernel, prompt_tiles=pt),
        grid=(n // tm,),
        in_specs=[*_split_rows(tm, ATTN_W, pt), *_split_rows(tm, POOL_W, pt),
                  pl.BlockSpec((tm, dm), lambda i: (i, ga_col)),
                  pl.BlockSpec((tm, dm), lambda i: (i, ga_col + 1)),
                  _layer_spec(w_ao.shape[1:], layer), _layer_spec(w_grp.shape[1:], layer),
                  _layer_spec(w_po.shape[1:], layer), _layer_spec((1, POOL_W), layer)],
        out_specs=pl.BlockSpec((tm, dm), lambda i: (i, 0)),
        out_shape=jax.ShapeDtypeStruct((n, dm), BF16),
        compiler_params=_params(),
        name="merge",
    )(attn_p, attn_s, d_p, d_s, proj, proj, w_ao, w_grp, w_po, scale)


def _first_lane_of_max(vals, lane):
    top = jnp.max(vals, axis=-1, keepdims=True)
    idx = jnp.min(jnp.where(vals == top, lane, LANE), axis=-1, keepdims=True)
    return top, idx


def _one_row(ref, row):
    return ref.at[pl.ds(row, 1)]


def _outproj_router_kernel(x_ref, m_ref, wo_ref, gf_ref, wr_ref, br_ref, x1_ref, h_ref, route_ref, cnt_ref,
                           run_ref, wr3_ref):
    step = pl.program_id(0)

    d = x_ref.shape[1]

    @pl.when(step == 0)
    def _():
        run_ref[...] = jnp.zeros_like(run_ref)
        w = wr_ref[...]
        w_hi = w.astype(BF16)
        wr3_ref[0:d, :] = w_hi
        wr3_ref[d:2 * d, :] = w_hi
        wr3_ref[2 * d:3 * d, :] = (w - w_hi.astype(F32)).astype(BF16)

    x1 = x_ref[...] + jnp.dot(m_ref[...], wo_ref[...], preferred_element_type=F32)
    x1_ref[...] = x1
    h = _rmsnorm(x1, gf_ref[...])
    h_ref[...] = h
    h_hi = h.astype(BF16)
    h_lo = (h - h_hi.astype(F32)).astype(BF16)
    logits = jnp.dot(jnp.concatenate([h_hi, h_lo, h_hi], axis=1), wr3_ref[...],
                     preferred_element_type=F32) + br_ref[...]
    lane = lax.broadcasted_iota(jnp.int32, logits.shape, 1)
    is_g = (lane >= N_EXPERTS) & (lane < N_EXPERTS + N_EXPERT_GROUPS)
    gmax, gidx = _first_lane_of_max(jnp.where(is_g, logits, -jnp.inf), lane)
    p_g = 1.0 / jnp.sum(jnp.where(is_g, jnp.exp(logits - gmax), 0.0), axis=-1, keepdims=True)
    in_grp = (lane // EXPERTS_PER_GROUP == gidx - N_EXPERTS) & (lane < N_EXPERTS)
    e1 = jnp.where(in_grp, logits, -jnp.inf)
    t1, i1 = _first_lane_of_max(e1, lane)
    t2, i2 = _first_lane_of_max(jnp.where(lane == i1, -jnp.inf, e1), lane)
    z = jnp.exp(t2 - t1)
    w1 = p_g / (1.0 + z)
    w2 = p_g * (z / (1.0 + z))
    oh1 = lane == i1
    oh2 = lane == i2
    picks = (oh1 | oh2).astype(F32)
    tm = picks.shape[0]
    earlier = (lax.broadcasted_iota(jnp.int32, (tm, tm), 1)
               < lax.broadcasted_iota(jnp.int32, (tm, tm), 0)).astype(BF16)
    before = jnp.dot(earlier, picks.astype(BF16), preferred_element_type=F32) + run_ref[...]
    r1 = jnp.sum(jnp.where(oh1, before, 0.0), axis=-1, keepdims=True)
    r2 = jnp.sum(jnp.where(oh2, before, 0.0), axis=-1, keepdims=True)
    run_ref[...] = run_ref[...] + jnp.sum(picks, axis=0, keepdims=True)
    cnt_ref[...] = jnp.broadcast_to(run_ref[...], cnt_ref.shape)
    route = jnp.zeros(logits.shape, F32)
    for k, val in enumerate((i1.astype(F32), i2.astype(F32), w1, w2, r1, r2)):
        route = jnp.where(lane == k, val, route)
    route_ref[...] = route


def _outproj_router(x, m, w_o, g_f, w_router, b_router, layer, *, tm=256):
    n, d = x.shape
    row = lambda width: pl.BlockSpec((tm, width), lambda i: (i, 0))
    return pl.pallas_call(
        _outproj_router_kernel,
        grid=(n // tm,),
        in_specs=[row(d), row(d), _layer_spec((d, d), layer), _layer_spec((1, d), layer),
                  _layer_spec((d, LANE), layer), _layer_spec((1, LANE), layer)],
        out_specs=[row(d), row(d), row(LANE), pl.BlockSpec((SUBLANE, LANE), lambda i: (0, 0))],
        out_shape=[jax.ShapeDtypeStruct((n, d), F32), jax.ShapeDtypeStruct((n, d), F32),
                   jax.ShapeDtypeStruct((n, LANE), F32), jax.ShapeDtypeStruct((SUBLANE, LANE), F32)],
        scratch_shapes=[pltpu.VMEM((1, LANE), F32), pltpu.VMEM((3 * d, LANE), BF16)],
        compiler_params=_params(),
        name="outproj_router",
    )(x, m, w_o, g_f, w_router, b_router)


def _dispatch_tables(route, counts_f):
    n = route.shape[0]
    a = n * TOPK_IN_GROUP
    n_blocks = a // MOE_BLOCK
    assert n_blocks * MOE_BLOCK == a
    n_items = n_blocks + N_EXPERTS - 1
    counts = counts_f[0, :N_EXPERTS].astype(jnp.int32)
    end = jnp.cumsum(counts)
    start = end - counts
    eid = route[:, :TOPK_IN_GROUP].astype(jnp.int32)
    rank = route[:, 2 * TOPK_IN_GROUP:3 * TOPK_IN_GROUP].astype(jnp.int32)
    onehot = eid[:, :, None] == jnp.arange(N_EXPERTS, dtype=jnp.int32)[None, None, :]
    dest = (jnp.sum(jnp.where(onehot, start[None, None, :], 0), axis=-1) + rank).reshape(a)

    first_blk = start // MOE_BLOCK
    items_e = jnp.where(counts > 0, (end - 1) // MOE_BLOCK - first_blk + 1, 0)
    item_end = jnp.cumsum(items_e)
    item_off = item_end - items_e
    total = item_end[-1]
    idx = jnp.arange(n_items, dtype=jnp.int32)
    e = jnp.minimum(jnp.sum((idx[:, None] >= item_end[None, :]).astype(jnp.int32), axis=1), N_EXPERTS - 1)
    pick = lambda tbl, ee: jnp.sum(jnp.where(ee[:, None] == jnp.arange(N_EXPERTS)[None, :], tbl[None, :], 0), axis=1)
    blk = pick(first_blk, e) + idx - pick(item_off, e)
    lo = jnp.maximum(pick(start, e), blk * MOE_BLOCK) - blk * MOE_BLOCK
    hi = jnp.minimum(pick(end, e), (blk + 1) * MOE_BLOCK) - blk * MOE_BLOCK
    valid = idx < total
    e_last = jnp.sum(jnp.where(idx == total - 1, e, 0))
    e = jnp.where(valid, e, e_last)
    blk = jnp.where(valid, blk, n_blocks - 1)
    lo = jnp.where(valid, lo, 0)
    hi = jnp.where(valid, hi, 0)
    first = jnp.concatenate([jnp.ones((1,), jnp.int32), (blk[1:] != blk[:-1]).astype(jnp.int32)])
    ids = jnp.arange(N_EXPERTS, dtype=jnp.int32)
    later = (ids[None, :] > ids[:, None]) & (counts[None, :] > 0)
    nxt_e = jnp.min(jnp.where(later, ids[None, :], N_EXPERTS), axis=1)
    nxt = pick(jnp.where(nxt_e < N_EXPERTS, nxt_e, -1), e)
    i32 = lambda t: t.astype(jnp.int32)
    return i32(dest), i32(blk), i32(e), i32(lo), i32(hi), first, i32(nxt)


def _dispatch_kernel(dest_ref, h_ref, xs_hbm, sem, *, tm):
    base = pl.program_id(0) * tm * TOPK_IN_GROUP

    def start(g, carry):
        r0 = pl.multiple_of(g * SUBLANE, SUBLANE)
        for u in range(SUBLANE):
            for k in range(TOPK_IN_GROUP):
                slot = dest_ref[base + (r0 + u) * TOPK_IN_GROUP + k]
                pltpu.make_async_copy(_one_row(h_ref, r0 + u), _one_row(xs_hbm, slot), sem).start(priority=k)
        return carry
    lax.fori_loop(0, tm // SUBLANE, start, 0)

    def wait(r, carry):
        for k in range(TOPK_IN_GROUP):
            pltpu.make_async_copy(_one_row(h_ref, 0), _one_row(xs_hbm, 0), sem).wait()
        return carry
    lax.fori_loop(0, tm, wait, 0, unroll=8)


def _dispatch(dest, h, *, tm=512):
    n, d = h.shape
    grid_spec = pltpu.PrefetchScalarGridSpec(
        num_scalar_prefetch=1,
        grid=(n // tm,),
        in_specs=[pl.BlockSpec((tm, d), lambda i, dr: (i, 0))],
        out_specs=pl.BlockSpec(memory_space=pl.ANY),
        scratch_shapes=[pltpu.SemaphoreType.DMA(())],
    )
    return pl.pallas_call(
        functools.partial(_dispatch_kernel, tm=tm),
        grid_spec=grid_spec,
        out_shape=jax.ShapeDtypeStruct((n * TOPK_IN_GROUP, d), F32),
        compiler_params=_params(),
        name="moe_dispatch",
    )(dest, h)


def _moe_ffn_kernel(blk_ref, e_ref, lo_ref, hi_ref, first_ref, nxt_ref, x_ref, wg_hbm, wu_hbm, wd_hbm, y_ref,
                    stage_g, stage_u, stage_d, wg_b, wu_b, wd_b, sems, cur_ref, *, layer):
    del blk_ref
    j = pl.program_id(0)

    def weight_copies(expert, s):
        return (pltpu.make_async_copy(wg_hbm.at[layer, expert], stage_g.at[s], sems.at[s]),
                pltpu.make_async_copy(wu_hbm.at[layer, expert], stage_u.at[s], sems.at[s]),
                pltpu.make_async_copy(wd_hbm.at[layer, expert], stage_d.at[s], sems.at[s]))

    @pl.when(j == 0)
    def _():
        cur_ref[0] = 1
        for c in weight_copies(e_ref[0], 0):
            c.start(priority=1)

    @pl.when((j == 0) | (e_ref[j] != e_ref[jnp.maximum(j - 1, 0)]))
    def _():
        s = 1 - cur_ref[0]
        cur_ref[0] = s

        @pl.when(nxt_ref[j] >= 0)
        def _():
            for c in weight_copies(nxt_ref[j], 1 - s):
                c.start(priority=1)

        for c in weight_copies(e_ref[j], s):
            c.wait()
        wg_b[...] = stage_g[s].astype(BF16)
        wu_b[...] = stage_u[s].astype(BF16)
        wd_b[...] = stage_d[s].astype(BF16)

    @pl.when(hi_ref[j] > lo_ref[j])
    def _():
        x = x_ref[...].astype(BF16)
        gate = jnp.dot(x, wg_b[...], preferred_element_type=F32)
        up = jnp.dot(x, wu_b[...], preferred_element_type=F32)
        act = (jax.nn.silu(gate) * up).astype(BF16)
        y = jnp.dot(act, wd_b[...], preferred_element_type=F32)
        row = lax.broadcasted_iota(jnp.int32, (MOE_BLOCK, 1), 0)
        y = jnp.where((row >= lo_ref[j]) & (row < hi_ref[j]), y, 0.0)

        @pl.when(first_ref[j] == 1)
        def _():
            y_ref[...] = y

        @pl.when(first_ref[j] == 0)
        def _():
            y_ref[...] = y_ref[...] + y


def _moe_ffn(xs, blk, e, lo, hi, first, nxt, w_gate, w_up, w_down, layer):
    a, d = xs.shape
    de = w_gate.shape[3]
    n_items = blk.shape[0]
    slab = pl.BlockSpec((MOE_BLOCK, d), lambda j, blk, *_: (blk[j], 0))
    hbm = pl.BlockSpec(memory_space=pl.ANY)
    grid_spec = pltpu.PrefetchScalarGridSpec(
        num_scalar_prefetch=6,
        grid=(n_items,),
        in_specs=[slab, hbm, hbm, hbm],
        out_specs=slab,
        scratch_shapes=[pltpu.VMEM((2, d, de), F32), pltpu.VMEM((2, d, de), F32), pltpu.VMEM((2, de, d), F32),
                        pltpu.VMEM((d, de), BF16), pltpu.VMEM((d, de), BF16), pltpu.VMEM((de, d), BF16),
                        pltpu.SemaphoreType.DMA((2,)), pltpu.SMEM((1,), jnp.int32)],
    )
    return pl.pallas_call(
        functools.partial(_moe_ffn_kernel, layer=layer),
        grid_spec=grid_spec,
        out_shape=jax.ShapeDtypeStruct((a, d), F32),
        compiler_params=_params(),
        name="moe_ffn",
    )(blk, e, lo, hi, first, nxt, xs, w_gate, w_up, w_down)


def _combine_ple_kernel(dest_ref, x_ref, route_ref, yb_hbm, pp_ref, ps_ref, gp_ref, wplg_ref, wple_ref, gfin_ref,
                        *rest, tm, prompt_tiles, final):
    outs, (ybuf, sems) = rest[:-2], rest[-2:]
    i = pl.program_id(0)
    nt = pl.num_programs(0)
    slot = i % 2

    def start_gather(tile, s):
        def body(g, carry):
            r0 = pl.multiple_of(g * SUBLANE, SUBLANE)
            for u in range(SUBLANE):
                for k in range(TOPK_IN_GROUP):
                    src = dest_ref[(tile * tm + r0 + u) * TOPK_IN_GROUP + k]
                    pltpu.make_async_copy(_one_row(yb_hbm, src), _one_row(ybuf.at[s, k], r0 + u),
                                          sems.at[s]).start(priority=k)
            return carry
        lax.fori_loop(0, tm // SUBLANE, body, 0)

    @pl.when(i == 0)
    def _():
        start_gather(0, 0)

    @pl.when(i + 1 < nt)
    def _():
        start_gather(i + 1, 1 - slot)

    def wait_body(r, carry):
        for k in range(TOPK_IN_GROUP):
            pltpu.make_async_copy(_one_row(yb_hbm, 0), _one_row(ybuf.at[slot, k], 0), sems.at[slot]).wait()
        return carry
    lax.fori_loop(0, tm, wait_body, 0, unroll=8)

    route = route_ref[...]
    moe = sum(ybuf[slot, k] * route[:, TOPK_IN_GROUP + k:TOPK_IN_GROUP + k + 1] for k in range(TOPK_IN_GROUP))
    x2 = x_ref[...] + moe
    hp = _rmsnorm(x2, gp_ref[...]).astype(BF16)
    gate = jax.nn.sigmoid(jnp.dot(hp, wplg_ref[...], preferred_element_type=F32))
    p = jnp.where(i < prompt_tiles, pp_ref[...], ps_ref[...])
    pe = jnp.dot(p.astype(BF16), wple_ref[...], preferred_element_type=F32)
    x3 = x2 + gate * pe
    if not final:
        outs[0][...] = x3
    else:
        y = _rmsnorm(x3, gfin_ref[...])

        @pl.when(i < prompt_tiles)
        def _():
            outs[0][...] = y

        @pl.when(i >= prompt_tiles)
        def _():
            outs[1][...] = y


def _split_rows(tm, width, pt, lead=()):
    none = (None,) * len(lead)
    return (pl.BlockSpec(none + (tm, width), lambda i, *_: lead + (jnp.minimum(i, pt - 1), 0)),
            pl.BlockSpec(none + (tm, width), lambda i, *_: lead + (jnp.maximum(i - pt, 0), 0)))


def _combine_ple(dest, x1, route, yb, p_prompt, p_sample, g_p, w_plg, w_ple, g_final, layer, *, final, tm=256):
    n, d = x1.shape
    n_prompt, pd = p_prompt.shape[1:]
    assert n_prompt % tm == 0 and (n - n_prompt) % tm == 0
    pt = n_prompt // tm
    row = lambda width: pl.BlockSpec((tm, width), lambda i, dr: (i, 0))
    if final:
        out_specs = list(_split_rows(tm, d, pt))
        out_shape = [jax.ShapeDtypeStruct((n_prompt, d), F32), jax.ShapeDtypeStruct((n - n_prompt, d), F32)]
    else:
        out_specs = [row(d)]
        out_shape = [jax.ShapeDtypeStruct((n, d), F32)]
    grid_spec = pltpu.PrefetchScalarGridSpec(
        num_scalar_prefetch=1,
        grid=(n // tm,),
        in_specs=[row(d), row(LANE), pl.BlockSpec(memory_space=pl.ANY), *_split_rows(tm, pd, pt, (layer,)),
                  _layer_spec((1, d), layer), _layer_spec((d, d), layer), _layer_spec((pd, d), layer),
                  pl.BlockSpec((1, d), lambda i, dr: (0, 0))],
        out_specs=out_specs,
        scratch_shapes=[pltpu.VMEM((2, TOPK_IN_GROUP, tm, d), F32), pltpu.SemaphoreType.DMA((2,))],
    )
    return pl.pallas_call(
        functools.partial(_combine_ple_kernel, tm=tm, prompt_tiles=pt, final=final),
        grid_spec=grid_spec,
        out_shape=out_shape,
        compiler_params=_params(),
        name="combine_ple",
    )(dest, x1, route, yb, p_prompt, p_sample, g_p, w_plg, w_ple, g_final.reshape(1, d))


def kernel(x_prompt, x_sample, cache_k, cache_v, state_pool, page_table, p_prompt, p_sample, w_in, w_attn_out, w_pool_out, w_out, w_pool_group, pool_scale, g_mix, g_ffn, g_ple, g_final, w_router_group, b_router_group, w_router_expert, b_router_expert, w_exp_gate, w_exp_up, w_exp_down, w_ple, w_ple_gate):
    nb_, seq, d = x_prompt.shape
    n_seq, n_new, _ = x_sample.shape
    depth = w_in.shape[0]
    n_p = nb_ * seq
    n_s = n_seq * n_new
    past_len = page_table.shape[1] * PAGE_SIZE
    u_col = 3 * ATTN_W // POOL_W
    ga_col = (3 * ATTN_W + POOL_W) // d
    assert 3 * ATTN_W == u_col * POOL_W and 3 * ATTN_W + POOL_W == ga_col * d

    x = jnp.concatenate([x_prompt.reshape(n_p, d), x_sample.reshape(n_s, d)], axis=0)
    outs = {name: [] for name in ("pool_p", "ks", "vs", "pool_s")}
    pad_new = lambda t: jnp.pad(t.reshape(n_seq, n_new, ATTN_W), ((0, 0), (0, SUBLANE - n_new), (0, 0)))
    heads_s = lambda t: t.reshape(n_seq, n_new, N_HEADS, HEAD_DIM).transpose(0, 2, 1, 3)

    per_layer_row = lambda g: g.reshape(depth, 1, g.shape[-1])
    w_in_b, w_ao_b, w_po_b, w_o_b, w_grp_b, w_plg_b, w_ple_b = (
        w.astype(BF16) for w in (w_in, w_attn_out, w_pool_out, w_out, w_pool_group, w_ple_gate, w_ple))
    lane_pad = LANE - N_EXPERTS - N_EXPERT_GROUPS
    w_router = jnp.pad(jnp.concatenate([w_router_expert, w_router_group], axis=2), ((0, 0), (0, 0), (0, lane_pad)))
    b_router = jnp.pad(jnp.concatenate([b_router_expert, b_router_group], axis=1),
                       ((0, 0), (0, lane_pad))).reshape(depth, 1, LANE)
    pp, ps = p_prompt.reshape(depth, n_p, -1), p_sample.reshape(depth, n_s, -1)
    kv_p = None

    for i in range(depth):
        proj = _inproj(x, per_layer_row(g_mix), w_in_b, i)
        attn_p, k_all, v_all = _moba_prompt(proj, nb_, seq, i, depth, kv_p)
        kv_p = (k_all, v_all)
        q_s, k_s, v_s = (proj[n_p:, c * ATTN_W:(c + 1) * ATTN_W] for c in range(3))
        attn_s = _moba_sample(pad_new(q_s), pad_new(k_s), pad_new(v_s), cache_k, cache_v, page_table, i, n_new)
        attn_s = attn_s[:, :n_new].reshape(n_s, ATTN_W).astype(BF16)
        u_p = proj[:n_p, 3 * ATTN_W:3 * ATTN_W + POOL_W].reshape(nb_, seq, POOL_W)
        u_s = proj[n_p:, 3 * ATTN_W:3 * ATTN_W + POOL_W].reshape(n_seq, n_new, POOL_W)
        d_p = _pool_prompt(proj, nb_, seq, u_col)
        d_s = _pool_sample(state_pool[i].transpose(1, 0, 2), u_s.transpose(1, 0, 2), past_len)
        d_s = d_s.transpose(1, 0, 2).reshape(n_s, POOL_W)
        m = _merge(attn_p, attn_s, d_p, d_s, proj, ga_col, w_ao_b, w_grp_b, w_po_b, per_layer_row(pool_scale), i)
        x1, h, route, counts = _outproj_router(x, m, w_o_b, per_layer_row(g_ffn), w_router, b_router, i)
        dest, blk, e, lo, hi, first, nxt = _dispatch_tables(route, counts)
        xs = _dispatch(dest, h)
        yb = _moe_ffn(xs, blk, e, lo, hi, first, nxt, w_exp_gate, w_exp_up, w_exp_down, i)
        res = _combine_ple(dest, x1, route, yb, pp, ps, per_layer_row(g_ple), w_plg_b, w_ple_b, g_final, i,
                           final=(i == depth - 1))
        x = res[0]

        outs["pool_p"].append(u_p[:, seq - POOL_HIST:])
        outs["ks"].append(heads_s(k_s))
        outs["vs"].append(heads_s(v_s))
        outs["pool_s"].append(jnp.concatenate([state_pool[i], u_s], axis=1)[:, n_new:])

    y_prompt = res[0].reshape(nb_, seq, d)
    y_sample = res[1].reshape(n_seq, n_new, d)
    stack = lambda name: jnp.stack(outs[name])
    return (y_prompt, y_sample, kv_p[0], kv_p[1], stack("pool_p"),
            stack("ks"), stack("vs"), stack("pool_s"))
```

```python
import functools

import jax
import jax.numpy as jnp
from jax import lax
from jax.experimental import pallas as pl
from jax.experimental.pallas import tpu as pltpu

F32 = jnp.float32
BF16 = jnp.bfloat16

LANE = 128
SUBLANE = 8
V7X_VMEM_LIMIT_BYTES = 56 * 1024 * 1024

N_HEADS = 8
HEAD_DIM = 128
ATTN_W = N_HEADS * HEAD_DIM
MOBA_BLOCK = 256
MOBA_TOPK = 3
PAGE_SIZE = 128
POOL_WINDOWS = (2, 4, 8, 16)
POOL_GW = 256
POOL_W = len(POOL_WINDOWS) * POOL_GW
POOL_HIST = max(POOL_WINDOWS) - 1
N_EXPERT_GROUPS = 4
EXPERTS_PER_GROUP = 8
N_EXPERTS = N_EXPERT_GROUPS * EXPERTS_PER_GROUP
TOPK_IN_GROUP = 2
MOE_BLOCK = 128
RMS_EPS = 1e-6
NEG = -1e30

_NT = (((1,), (1,)), ((), ()))
_TN = (((0,), (0,)), ((), ()))
_HI = lax.Precision.HIGHEST


def _params():
    return pltpu.CompilerParams(vmem_limit_bytes=V7X_VMEM_LIMIT_BYTES)


def _rmsnorm(x, g):
    return x * lax.rsqrt(jnp.mean(x * x, axis=-1, keepdims=True) + RMS_EPS) * g


def _inproj_kernel(x_ref, g_ref, w_ref, o_ref, xn_ref):
    @pl.when(pl.program_id(1) == 0)
    def _():
        xn_ref[...] = _rmsnorm(x_ref[...], g_ref[...]).astype(BF16)

    o_ref[...] = jnp.dot(xn_ref[...], w_ref[...], preferred_element_type=F32)


def _layer_spec(shape, layer):
    return pl.BlockSpec((None,) + tuple(shape), lambda *_: (layer,) + (0,) * len(shape))


def _row_tile(n, target):
    best = SUBLANE
    for t in range(SUBLANE, target + 1, SUBLANE):
        if n % t == 0:
            best = t
    return best


def _inproj(x, g_all, w_all_bf16, layer, *, tn=1024):
    n, d = x.shape
    pw = w_all_bf16.shape[2]
    tm = _row_tile(n, 1152)
    return pl.pallas_call(
        _inproj_kernel,
        grid=(n // tm, pw // tn),
        in_specs=[pl.BlockSpec((tm, d), lambda i, j: (i, 0)),
                  _layer_spec((1, d), layer),
                  pl.BlockSpec((None, d, tn), lambda i, j: (layer, 0, j))],
        out_specs=pl.BlockSpec((tm, tn), lambda i, j: (i, j)),
        out_shape=jax.ShapeDtypeStruct((n, pw), F32),
        scratch_shapes=[pltpu.VMEM((tm, d), BF16)],
        compiler_params=_params(),
        name="inproj",
    )(x, g_all, w_all_bf16)


def _moba_prompt_kernel(q_ref, k_ref, v_ref, o_ref, ko_ref, vo_ref, s_ref, *, nb):
    blk = MOBA_BLOCK
    scale = HEAD_DIM ** -0.5
    k_sel = min(MOBA_TOPK, nb - 1)
    k = k_ref[...]
    v = v_ref[...]
    ko_ref[...] = jnp.broadcast_to(k, ko_ref.shape)
    vo_ref[...] = jnp.broadcast_to(v, vo_ref.shape)
    kb = k.astype(BF16)
    vb = v.astype(BF16)
    km = jnp.concatenate([jnp.mean(k[j * blk:(j + 1) * blk], axis=0, keepdims=True) for j in range(nb)]
                         + [jnp.zeros((LANE - nb, HEAD_DIM), F32)], axis=0)
    km_hi = km.astype(BF16)
    km3 = jnp.concatenate([km_hi, km_hi, (km - km_hi.astype(F32)).astype(BF16)], axis=1)
    row = lax.broadcasted_iota(jnp.int32, (blk, blk), 0)
    col = lax.broadcasted_iota(jnp.int32, (blk, blk), 1)
    causal = col <= row
    for n in range(nb):
        q = q_ref[n * blk:(n + 1) * blk, :]
        qs = (q * scale).astype(BF16)
        sel = None
        if n > k_sel:
            q_hi = q.astype(BF16)
            q3 = jnp.concatenate([q_hi, (q - q_hi.astype(F32)).astype(BF16), q_hi], axis=1)
            gates = lax.dot_general(q3, km3, _NT, preferred_element_type=F32)
            g = [jnp.broadcast_to(gates[:, j:j + 1], (blk, LANE)) for j in range(n)]
            rank = [jnp.zeros((blk, LANE), F32) for _ in range(n)]
            for j in range(n):
                for jp in range(j):
                    first_wins = (g[jp] >= g[j]).astype(F32)
                    rank[j] = rank[j] + first_wins
                    rank[jp] = rank[jp] + (1.0 - first_wins)
            sel = [r < k_sel for r in rank]
        for j in range(n + 1):
            s = lax.dot_general(qs, kb[j * blk:(j + 1) * blk], _NT, preferred_element_type=F32)
            if j == n:
                s_ref[:, j * blk:(j + 1) * blk] = jnp.where(causal, s, NEG)
            elif sel is None:
                s_ref[:, j * blk:(j + 1) * blk] = s
            else:
                for half in range(blk // LANE):
                    lo = half * LANE
                    s_ref[:, j * blk + lo:j * blk + lo + LANE] = jnp.where(sel[j], s[:, lo:lo + LANE], NEG)
        nk = (n + 1) * blk
        s_all = s_ref[:, :nk]
        m = jnp.max(s_all, axis=-1, keepdims=True)
        p = jnp.exp(s_all - m)
        l = jnp.sum(p, axis=-1, keepdims=True)
        o = jnp.dot(p.astype(BF16), vb[:nk], preferred_element_type=F32) / l
        o_ref[n * blk:(n + 1) * blk, :] = o.astype(o_ref.dtype)


def _moba_prompt_kernel_aliased(q_ref, k_ref, v_ref, kprev_ref, vprev_ref, *rest, nb):
    del kprev_ref, vprev_ref
    _moba_prompt_kernel(q_ref, k_ref, v_ref, *rest, nb=nb)


def _moba_prompt(proj, n_batch, seq, layer, depth, kv_prev):
    nb = seq // MOBA_BLOCK
    assert nb * MOBA_BLOCK == seq
    blk_spec = lambda off: pl.BlockSpec((seq, HEAD_DIM), lambda b, h: (b, off + h))
    kv_shape = jax.ShapeDtypeStruct((depth, n_batch, N_HEADS, seq, HEAD_DIM), F32)
    in_specs = [blk_spec(0), blk_spec(N_HEADS), blk_spec(2 * N_HEADS)]
    args = [proj, proj, proj]
    body, aliases = _moba_prompt_kernel, {}
    if kv_prev is None:
        kv_out = pl.BlockSpec((depth, None, None, seq, HEAD_DIM), lambda b, h: (0, b, h, 0, 0))
    else:
        kv_out = pl.BlockSpec((None, None, None, seq, HEAD_DIM), lambda b, h: (layer, b, h, 0, 0))
        in_specs += [pl.BlockSpec(memory_space=pl.ANY)] * 2
        args += list(kv_prev)
        body, aliases = _moba_prompt_kernel_aliased, {3: 1, 4: 2}
    return pl.pallas_call(
        functools.partial(body, nb=nb),
        grid=(n_batch, N_HEADS),
        in_specs=in_specs,
        out_specs=[pl.BlockSpec((seq, HEAD_DIM), lambda b, h: (b, h)), kv_out, kv_out],
        out_shape=[jax.ShapeDtypeStruct((n_batch * seq, ATTN_W), BF16), kv_shape, kv_shape],
        scratch_shapes=[pltpu.VMEM((MOBA_BLOCK, seq), F32)],
        input_output_aliases=aliases,
        compiler_params=_params(),
        name="moba_prompt",
    )(*args)


def _moba_sample_kernel(pt_ref, q_ref, kn_ref, vn_ref, *refs, n_pages, n_new):
    del pt_ref
    k_refs = refs[:n_pages]
    v_refs = refs[n_pages:2 * n_pages]
    o_ref = refs[2 * n_pages]
    st_ref, vcat_ref, km_ref = refs[2 * n_pages + 1:]
    ps = PAGE_SIZE
    ppb = MOBA_BLOCK // ps
    nbp = n_pages // ppb
    k_sel = min(MOBA_TOPK, nbp)
    scale = HEAD_DIM ** -0.5

    q8 = q_ref[0]
    qrep = jnp.concatenate([q8] * (LANE // SUBLANE), axis=0)
    cc = lax.broadcasted_iota(jnp.int32, (LANE, ATTN_W), 0)
    ll = lax.broadcasted_iota(jnp.int32, (LANE, ATTN_W), 1)
    qrep = jnp.where(ll // HEAD_DIM == cc // SUBLANE, qrep, 0.0)
    qs_b = (qrep * scale).astype(BF16)

    acc = None
    for p in range(n_pages):
        kcat = jnp.concatenate([k_refs[p][h] for h in range(N_HEADS)], axis=1)
        vcat = jnp.concatenate([v_refs[p][h] for h in range(N_HEADS)], axis=1)
        st_ref[p * ps:(p + 1) * ps, :] = lax.dot_general(kcat.astype(BF16), qs_b, _NT,
                                                         preferred_element_type=F32)
        vcat_ref[p * ps:(p + 1) * ps, :] = vcat.astype(BF16)
        colsum = jnp.sum(kcat, axis=0, keepdims=True)
        acc = colsum if p % ppb == 0 else acc + colsum
        if p % ppb == ppb - 1:
            j = p // ppb
            km_ref[j:j + 1, :] = acc * (1.0 / MOBA_BLOCK)

    gate = lax.dot_general(km_ref[...], qrep, _NT, precision=_HI, preferred_element_type=F32)
    jidx = lax.broadcasted_iota(jnp.int32, (nbp, LANE), 0)
    sel = []
    for j in range(nbp):
        gj = gate[j:j + 1, :]
        beats = (gate > gj) | ((gate == gj) & (jidx < j))
        sel.append(jnp.sum(beats.astype(F32), axis=0, keepdims=True) < k_sel)

    kn8 = kn_ref[0]
    vn8 = vn_ref[0]
    sn = lax.dot_general(kn8.astype(BF16), qs_b, _NT, preferred_element_type=F32)
    s_i = lax.broadcasted_iota(jnp.int32, (SUBLANE, LANE), 0)
    t_c = lax.broadcasted_iota(jnp.int32, (SUBLANE, LANE), 1) % SUBLANE
    sn = jnp.where((s_i <= t_c) & (s_i < n_new), sn, NEG)
    m = jnp.max(sn, axis=0, keepdims=True)
    for j in range(nbp):
        mj = jnp.max(st_ref[j * MOBA_BLOCK:(j + 1) * MOBA_BLOCK, :], axis=0, keepdims=True)
        m = jnp.where(sel[j], jnp.maximum(m, mj), m)
    pn = jnp.exp(sn - m)
    l = jnp.sum(pn, axis=0, keepdims=True)
    for j in range(nbp):
        rows = slice(j * MOBA_BLOCK, (j + 1) * MOBA_BLOCK)
        pj = jnp.where(sel[j], jnp.exp(st_ref[rows, :] - m), 0.0)
        l = l + jnp.sum(pj, axis=0, keepdims=True)
        st_ref[rows, :] = pj
    inv = 1.0 / l
    pt = (st_ref[...] * inv).astype(BF16)
    o_full = lax.dot_general(pt, vcat_ref[...], _TN, preferred_element_type=F32)
    o_full = o_full + lax.dot_general((pn * inv).astype(BF16), vn8.astype(BF16), _TN,
                                      preferred_element_type=F32)
    o_ref[0] = jnp.concatenate(
        [o_full[h * SUBLANE:(h + 1) * SUBLANE, h * HEAD_DIM:(h + 1) * HEAD_DIM] for h in range(N_HEADS)], axis=1)


def _moba_sample(q8, kn8, vn8, cache_k, cache_v, page_table, layer, n_new):
    assert n_new <= SUBLANE
    n_seq, n_pages = page_table.shape
    assert (n_pages * PAGE_SIZE) % MOBA_BLOCK == 0
    tok_spec = pl.BlockSpec((1, SUBLANE, ATTN_W), lambda s, pt: (s, 0, 0))

    def page_spec(p):
        return pl.BlockSpec((None, None, N_HEADS, PAGE_SIZE, HEAD_DIM),
                            lambda s, pt: (layer, pt[s, p], 0, 0, 0))

    past = n_pages * PAGE_SIZE
    grid_spec = pltpu.PrefetchScalarGridSpec(
        num_scalar_prefetch=1,
        grid=(n_seq,),
        in_specs=[tok_spec, tok_spec, tok_spec] + [page_spec(p) for p in range(n_pages)] * 2,
        out_specs=tok_spec,
        scratch_shapes=[pltpu.VMEM((past, LANE), F32), pltpu.VMEM((past, ATTN_W), BF16),
                        pltpu.VMEM((past // MOBA_BLOCK, ATTN_W), F32)],
    )
    return pl.pallas_call(
        functools.partial(_moba_sample_kernel, n_pages=n_pages, n_new=n_new),
        grid_spec=grid_spec,
        out_shape=jax.ShapeDtypeStruct((n_seq, SUBLANE, ATTN_W), F32),
        compiler_params=_params(),
        name="moba_sample",
    )(page_table, q8, kn8, vn8, *([cache_k] * n_pages), *([cache_v] * n_pages))


def _pool_prompt_kernel(halo_ref, u_ref, d_ref, *, tile, halo):
    t = pl.program_id(1)
    u = u_ref[...]
    prev = jnp.where(t > 0, halo_ref[...], 0.0)
    ucat = jnp.concatenate([prev, u], axis=0)
    hi = ucat.astype(BF16)
    lo = (ucat - hi.astype(F32)).astype(BF16)
    r = lax.broadcasted_iota(jnp.int32, (tile, halo + tile), 0) + halo
    c = lax.broadcasted_iota(jnp.int32, (tile, halo + tile), 1)
    pos = t * tile + lax.broadcasted_iota(jnp.int32, (tile, 1), 0)
    for g, w in enumerate(POOL_WINDOWS):
        band = ((c <= r) & (c > r - w)).astype(BF16)
        sl = slice(g * POOL_GW, (g + 1) * POOL_GW)
        acc = (jnp.dot(band, hi[:, sl], preferred_element_type=F32)
               + jnp.dot(band, lo[:, sl], preferred_element_type=F32))
        cnt = jnp.minimum(pos + 1, w).astype(F32)
        d_ref[:, sl] = (acc / cnt - u[:, sl]).astype(d_ref.dtype)


def _pool_prompt(proj, n_batch, seq, u_col, *, tile=256, halo=128):
    assert halo >= POOL_HIST and tile % halo == 0 and seq % tile == 0
    tiles = seq // tile
    per = tile // halo
    return pl.pallas_call(
        functools.partial(_pool_prompt_kernel, tile=tile, halo=halo),
        grid=(n_batch, tiles),
        in_specs=[pl.BlockSpec((halo, POOL_W), lambda b, t: (jnp.maximum((b * tiles + t) * per - 1, 0), u_col)),
                  pl.BlockSpec((tile, POOL_W), lambda b, t: (b * tiles + t, u_col))],
        out_specs=pl.BlockSpec((tile, POOL_W), lambda b, t: (b * tiles + t, 0)),
        out_shape=jax.ShapeDtypeStruct((n_batch * seq, POOL_W), BF16),
        compiler_params=_params(),
        name="pool_prompt",
    )(proj, proj)


def _pool_sample_kernel(h_ref, u_ref, d_ref, *, n_hist, n_new, pos0):
    for g, w in enumerate(POOL_WINDOWS):
        sl = slice(g * POOL_GW, (g + 1) * POOL_GW)
        rows = [h_ref[j, :, sl] for j in range(n_hist)] + [u_ref[t, :, sl] for t in range(n_new)]
        for t in range(n_new):
            ti = n_hist + t
            first = max(ti + 1 - w, 0)
            acc = rows[first]
            for j in range(first + 1, ti + 1):
                acc = acc + rows[j]
            cnt = float(min(pos0 + t + 1, w))
            d_ref[t, :, sl] = (acc / cnt - rows[ti]).astype(d_ref.dtype)


def _pool_sample(hist_t, u_t, pos0, *, chunk=32):
    n_hist, n_seq, _ = hist_t.shape
    n_new = u_t.shape[0]
    return pl.pallas_call(
        functools.partial(_pool_sample_kernel, n_hist=n_hist, n_new=n_new, pos0=pos0),
        grid=(n_seq // chunk,),
        in_specs=[pl.BlockSpec((n_hist, chunk, POOL_W), lambda i: (0, i, 0)),
                  pl.BlockSpec((n_new, chunk, POOL_W), lambda i: (0, i, 0))],
        out_specs=pl.BlockSpec((n_new, chunk, POOL_W), lambda i: (0, i, 0)),
        out_shape=jax.ShapeDtypeStruct((n_new, n_seq, POOL_W), BF16),
        compiler_params=_params(),
        name="pool_sample",
    )(hist_t, u_t)


def _merge_kernel(ap_ref, as_ref, dp_ref, ds_ref, ga_ref, gb_ref, wao_ref, wgrp_ref, wpo_ref, sc_ref, m_ref, *,
                  prompt_tiles):
    is_prompt = pl.program_id(0) < prompt_tiles
    attn = jnp.where(is_prompt, ap_ref[...], as_ref[...])
    dd = jnp.where(is_prompt, dp_ref[...], ds_ref[...])
    a = jnp.dot(attn, wao_ref[...], preferred_element_type=F32)
    ys = [jnp.dot(dd[:, g * POOL_GW:(g + 1) * POOL_GW], wgrp_ref[g], preferred_element_type=F32)
          for g in range(len(POOL_WINDOWS))]
    pooled = (jnp.concatenate(ys, axis=1) * sc_ref[...]).astype(BF16)
    b = jnp.dot(pooled, wpo_ref[...], preferred_element_type=F32)
    m = jax.nn.sigmoid(ga_ref[...]) * a + jax.nn.sigmoid(gb_ref[...]) * b
    m_ref[...] = m.astype(m_ref.dtype)


def _merge(attn_p, attn_s, d_p, d_s, proj, ga_col, w_ao, w_grp, w_po, scale, layer, *, tm=256):
    n = proj.shape[0]
    dm = w_ao.shape[2]
    assert attn_p.shape[0] % tm == 0 and attn_s.shape[0] % tm == 0
    pt = attn_p.shape[0] // tm
    return pl.pallas_call(
        functools.partial(_merge_kernel, prompt_tiles=pt),
        grid=(n // tm,),
        in_specs=[*_split_rows(tm, ATTN_W, pt), *_split_rows(tm, POOL_W, pt),
                  pl.BlockSpec((tm, dm), lambda i: (i, ga_col)),
                  pl.BlockSpec((tm, dm), lambda i: (i, ga_col + 1)),
                  _layer_spec(w_ao.shape[1:], layer), _layer_spec(w_grp.shape[1:], layer),
                  _layer_spec(w_po.shape[1:], layer), _layer_spec((1, POOL_W), layer)],
        out_specs=pl.BlockSpec((tm, dm), lambda i: (i, 0)),
        out_shape=jax.ShapeDtypeStruct((n, dm), BF16),
        compiler_params=_params(),
        name="merge",
    )(attn_p, attn_s, d_p, d_s, proj, proj, w_ao, w_grp, w_po, scale)


def _first_lane_of_max(vals, lane):
    top = jnp.max(vals, axis=-1, keepdims=True)
    idx = jnp.min(jnp.where(vals == top, lane, LANE), axis=-1, keepdims=True)
    return top, idx


def _one_row(ref, row):
    return ref.at[pl.ds(row, 1)]


def _outproj_router_kernel(x_ref, m_ref, wo_ref, gf_ref, wr_ref, br_ref, x1_ref, h_ref, route_ref, cnt_ref,
                           run_ref, wr3_ref):
    step = pl.program_id(0)

    d = x_ref.shape[1]

    @pl.when(step == 0)
    def _():
        run_ref[...] = jnp.zeros_like(run_ref)
        w = wr_ref[...]
        w_hi = w.astype(BF16)
        wr3_ref[:, 0:LANE] = w_hi
        wr3_ref[:, LANE:2 * LANE] = (w - w_hi.astype(F32)).astype(BF16)

    x1 = x_ref[...] + jnp.dot(m_ref[...], wo_ref[...], preferred_element_type=F32)
    x1_ref[...] = x1
    h = _rmsnorm(x1, gf_ref[...])
    h_ref[...] = h
    h_hi = h.astype(BF16)
    h_lo = (h - h_hi.astype(F32)).astype(BF16)
    hi_both = jnp.dot(h_hi, wr3_ref[...], preferred_element_type=F32)
    logits = (hi_both[:, :LANE] + hi_both[:, LANE:]
              + jnp.dot(h_lo, wr3_ref[:, 0:LANE], preferred_element_type=F32) + br_ref[...])
    lane = lax.broadcasted_iota(jnp.int32, logits.shape, 1)
    is_g = (lane >= N_EXPERTS) & (lane < N_EXPERTS + N_EXPERT_GROUPS)
    gmax, gidx = _first_lane_of_max(jnp.where(is_g, logits, -jnp.inf), lane)
    p_g = 1.0 / jnp.sum(jnp.where(is_g, jnp.exp(logits - gmax), 0.0), axis=-1, keepdims=True)
    in_grp = (lane // EXPERTS_PER_GROUP == gidx - N_EXPERTS) & (lane < N_EXPERTS)
    e1 = jnp.where(in_grp, logits, -jnp.inf)
    t1, i1 = _first_lane_of_max(e1, lane)
    t2, i2 = _first_lane_of_max(jnp.where(lane == i1, -jnp.inf, e1), lane)
    z = jnp.exp(t2 - t1)
    w1 = p_g / (1.0 + z)
    w2 = p_g * (z / (1.0 + z))
    oh1 = lane == i1
    oh2 = lane == i2
    picks = (oh1 | oh2).astype(F32)
    tm = picks.shape[0]
    earlier = (lax.broadcasted_iota(jnp.int32, (tm, tm), 1)
               < lax.broadcasted_iota(jnp.int32, (tm, tm), 0)).astype(BF16)
    before = jnp.dot(earlier, picks.astype(BF16), preferred_element_type=F32) + run_ref[...]
    r1 = jnp.sum(jnp.where(oh1, before, 0.0), axis=-1, keepdims=True)
    r2 = jnp.sum(jnp.where(oh2, before, 0.0), axis=-1, keepdims=True)
    run_ref[...] = run_ref[...] + jnp.sum(picks, axis=0, keepdims=True)
    cnt_ref[...] = jnp.broadcast_to(run_ref[...], cnt_ref.shape)
    route = jnp.zeros(logits.shape, F32)
    for k, val in enumerate((i1.astype(F32), i2.astype(F32), w1, w2, r1, r2)):
        route = jnp.where(lane == k, val, route)
    route_ref[...] = route


def _outproj_router(x, m, w_o, g_f, w_router, b_router, layer, *, tm=256):
    n, d = x.shape
    row = lambda width: pl.BlockSpec((tm, width), lambda i: (i, 0))
    return pl.pallas_call(
        _outproj_router_kernel,
        grid=(n // tm,),
        in_specs=[row(d), row(d), _layer_spec((d, d), layer), _layer_spec((1, d), layer),
                  _layer_spec((d, LANE), layer), _layer_spec((1, LANE), layer)],
        out_specs=[row(d), row(d), row(LANE), pl.BlockSpec((SUBLANE, LANE), lambda i: (0, 0))],
        out_shape=[jax.ShapeDtypeStruct((n, d), F32), jax.ShapeDtypeStruct((n, d), F32),
                   jax.ShapeDtypeStruct((n, LANE), F32), jax.ShapeDtypeStruct((SUBLANE, LANE), F32)],
        scratch_shapes=[pltpu.VMEM((1, LANE), F32), pltpu.VMEM((d, 2 * LANE), BF16)],
        compiler_params=_params(),
        name="outproj_router",
    )(x, m, w_o, g_f, w_router, b_router)


def _dispatch_tables(route, counts_f):
    n = route.shape[0]
    a = n * TOPK_IN_GROUP
    n_blocks = a // MOE_BLOCK
    assert n_blocks * MOE_BLOCK == a
    n_items = n_blocks + N_EXPERTS - 1
    counts = counts_f[0, :N_EXPERTS].astype(jnp.int32)
    end = jnp.cumsum(counts)
    start = end - counts
    eid = route[:, :TOPK_IN_GROUP].astype(jnp.int32)
    rank = route[:, 2 * TOPK_IN_GROUP:3 * TOPK_IN_GROUP].astype(jnp.int32)
    onehot = eid[:, :, None] == jnp.arange(N_EXPERTS, dtype=jnp.int32)[None, None, :]
    dest = (jnp.sum(jnp.where(onehot, start[None, None, :], 0), axis=-1) + rank).reshape(a)

    first_blk = start // MOE_BLOCK
    items_e = jnp.where(counts > 0, (end - 1) // MOE_BLOCK - first_blk + 1, 0)
    item_end = jnp.cumsum(items_e)
    item_off = item_end - items_e
    total = item_end[-1]
    idx = jnp.arange(n_items, dtype=jnp.int32)
    e = jnp.minimum(jnp.sum((idx[:, None] >= item_end[None, :]).astype(jnp.int32), axis=1), N_EXPERTS - 1)
    pick = lambda tbl, ee: jnp.sum(jnp.where(ee[:, None] == jnp.arange(N_EXPERTS)[None, :], tbl[None, :], 0), axis=1)
    blk = pick(first_blk, e) + idx - pick(item_off, e)
    lo = jnp.maximum(pick(start, e), blk * MOE_BLOCK) - blk * MOE_BLOCK
    hi = jnp.minimum(pick(end, e), (blk + 1) * MOE_BLOCK) - blk * MOE_BLOCK
    valid = idx < total
    e_last = jnp.sum(jnp.where(idx == total - 1, e, 0))
    e = jnp.where(valid, e, e_last)
    blk = jnp.where(valid, blk, n_blocks - 1)
    lo = jnp.where(valid, lo, 0)
    hi = jnp.where(valid, hi, 0)
    first = jnp.concatenate([jnp.ones((1,), jnp.int32), (blk[1:] != blk[:-1]).astype(jnp.int32)])
    ids = jnp.arange(N_EXPERTS, dtype=jnp.int32)
    later = (ids[None, :] > ids[:, None]) & (counts[None, :] > 0)
    nxt_e = jnp.min(jnp.where(later, ids[None, :], N_EXPERTS), axis=1)
    nxt = pick(jnp.where(nxt_e < N_EXPERTS, nxt_e, -1), e)
    i32 = lambda t: t.astype(jnp.int32)
    return i32(dest), i32(blk), i32(e), i32(lo), i32(hi), first, i32(nxt)


def _dispatch_kernel(dest_ref, h_ref, xs_hbm, sem, *, tm):
    base = pl.program_id(0) * tm * TOPK_IN_GROUP

    def start(g, carry):
        r0 = pl.multiple_of(g * SUBLANE, SUBLANE)
        for u in range(SUBLANE):
            for k in range(TOPK_IN_GROUP):
                slot = dest_ref[base + (r0 + u) * TOPK_IN_GROUP + k]
                pltpu.make_async_copy(_one_row(h_ref, r0 + u), _one_row(xs_hbm, slot), sem).start(priority=k)
        return carry
    lax.fori_loop(0, tm // SUBLANE, start, 0)

    def wait(r, carry):
        for k in range(TOPK_IN_GROUP):
            pltpu.make_async_copy(_one_row(h_ref, 0), _one_row(xs_hbm, 0), sem).wait()
        return carry
    lax.fori_loop(0, tm, wait, 0, unroll=8)


def _dispatch(dest, h, *, tm=512):
    n, d = h.shape
    grid_spec = pltpu.PrefetchScalarGridSpec(
        num_scalar_prefetch=1,
        grid=(n // tm,),
        in_specs=[pl.BlockSpec((tm, d), lambda i, dr: (i, 0))],
        out_specs=pl.BlockSpec(memory_space=pl.ANY),
        scratch_shapes=[pltpu.SemaphoreType.DMA(())],
    )
    return pl.pallas_call(
        functools.partial(_dispatch_kernel, tm=tm),
        grid_spec=grid_spec,
        out_shape=jax.ShapeDtypeStruct((n * TOPK_IN_GROUP, d), F32),
        compiler_params=_params(),
        name="moe_dispatch",
    )(dest, h)


def _moe_ffn_kernel(blk_ref, e_ref, lo_ref, hi_ref, first_ref, nxt_ref, x_ref, wg_hbm, wu_hbm, wd_hbm, y_ref,
                    stage_g, stage_u, stage_d, wg_b, wu_b, wd_b, sems, cur_ref, *, layer):
    del blk_ref
    j = pl.program_id(0)

    def weight_copies(expert, s):
        return (pltpu.make_async_copy(wg_hbm.at[layer, expert], stage_g.at[s], sems.at[s]),
                pltpu.make_async_copy(wu_hbm.at[layer, expert], stage_u.at[s], sems.at[s]),
                pltpu.make_async_copy(wd_hbm.at[layer, expert], stage_d.at[s], sems.at[s]))

    @pl.when(j == 0)
    def _():
        cur_ref[0] = 1
        for c in weight_copies(e_ref[0], 0):
            c.start(priority=1)

    @pl.when((j == 0) | (e_ref[j] != e_ref[jnp.maximum(j - 1, 0)]))
    def _():
        s = 1 - cur_ref[0]
        cur_ref[0] = s

        @pl.when(nxt_ref[j] >= 0)
        def _():
            for c in weight_copies(nxt_ref[j], 1 - s):
                c.start(priority=1)

        for c in weight_copies(e_ref[j], s):
            c.wait()
        wg_b[...] = stage_g[s].astype(BF16)
        wu_b[...] = stage_u[s].astype(BF16)
        wd_b[...] = stage_d[s].astype(BF16)

    @pl.when(hi_ref[j] > lo_ref[j])
    def _():
        x = x_ref[...].astype(BF16)
        gate = jnp.dot(x, wg_b[...], preferred_element_type=F32)
        up = jnp.dot(x, wu_b[...], preferred_element_type=F32)
        act = (jax.nn.silu(gate) * up).astype(BF16)
        y = jnp.dot(act, wd_b[...], preferred_element_type=F32)
        row = lax.broadcasted_iota(jnp.int32, (MOE_BLOCK, 1), 0)
        y = jnp.where((row >= lo_ref[j]) & (row < hi_ref[j]), y, 0.0)

        @pl.when(first_ref[j] == 1)
        def _():
            y_ref[...] = y

        @pl.when(first_ref[j] == 0)
        def _():
            y_ref[...] = y_ref[...] + y


def _moe_ffn(xs, blk, e, lo, hi, first, nxt, w_gate, w_up, w_down, layer):
    a, d = xs.shape
    de = w_gate.shape[3]
    n_items = blk.shape[0]
    slab = pl.BlockSpec((MOE_BLOCK, d), lambda j, blk, *_: (blk[j], 0))
    hbm = pl.BlockSpec(memory_space=pl.ANY)
    grid_spec = pltpu.PrefetchScalarGridSpec(
        num_scalar_prefetch=6,
        grid=(n_items,),
        in_specs=[slab, hbm, hbm, hbm],
        out_specs=slab,
        scratch_shapes=[pltpu.VMEM((2, d, de), F32), pltpu.VMEM((2, d, de), F32), pltpu.VMEM((2, de, d), F32),
                        pltpu.VMEM((d, de), BF16), pltpu.VMEM((d, de), BF16), pltpu.VMEM((de, d), BF16),
                        pltpu.SemaphoreType.DMA((2,)), pltpu.SMEM((1,), jnp.int32)],
    )
    return pl.pallas_call(
        functools.partial(_moe_ffn_kernel, layer=layer),
        grid_spec=grid_spec,
        out_shape=jax.ShapeDtypeStruct((a, d), F32),
        compiler_params=_params(),
        name="moe_ffn",
    )(blk, e, lo, hi, first, nxt, xs, w_gate, w_up, w_down)


def _combine_ple_kernel(dest_ref, x_ref, route_ref, yb_hbm, pp_ref, ps_ref, gp_ref, wplg_ref, wple_ref, gfin_ref,
                        *rest, tm, prompt_tiles, final):
    outs, (ybuf, sems) = rest[:-2], rest[-2:]
    i = pl.program_id(0)
    nt = pl.num_programs(0)
    slot = i % 2

    def start_gather(tile, s):
        def body(g, carry):
            r0 = pl.multiple_of(g * SUBLANE, SUBLANE)
            for u in range(SUBLANE):
                for k in range(TOPK_IN_GROUP):
                    src = dest_ref[(tile * tm + r0 + u) * TOPK_IN_GROUP + k]
                    pltpu.make_async_copy(_one_row(yb_hbm, src), _one_row(ybuf.at[s, k], r0 + u),
                                          sems.at[s]).start(priority=k)
            return carry
        lax.fori_loop(0, tm // SUBLANE, body, 0)

    @pl.when(i == 0)
    def _():
        start_gather(0, 0)

    @pl.when(i + 1 < nt)
    def _():
        start_gather(i + 1, 1 - slot)

    def wait_body(r, carry):
        for k in range(TOPK_IN_GROUP):
            pltpu.make_async_copy(_one_row(yb_hbm, 0), _one_row(ybuf.at[slot, k], 0), sems.at[slot]).wait()
        return carry
    lax.fori_loop(0, tm, wait_body, 0, unroll=8)

    route = route_ref[...]
    moe = sum(ybuf[slot, k] * route[:, TOPK_IN_GROUP + k:TOPK_IN_GROUP + k + 1] for k in range(TOPK_IN_GROUP))
    x2 = x_ref[...] + moe
    hp = _rmsnorm(x2, gp_ref[...]).astype(BF16)
    gate = jax.nn.sigmoid(jnp.dot(hp, wplg_ref[...], preferred_element_type=F32))
    p = jnp.where(i < prompt_tiles, pp_ref[...], ps_ref[...])
    pe = jnp.dot(p.astype(BF16), wple_ref[...], preferred_element_type=F32)
    x3 = x2 + gate * pe
    if not final:
        outs[0][...] = x3
    else:
        y = _rmsnorm(x3, gfin_ref[...])

        @pl.when(i < prompt_tiles)
        def _():
            outs[0][...] = y

        @pl.when(i >= prompt_tiles)
        def _():
            outs[1][...] = y


def _split_rows(tm, width, pt, lead=()):
    none = (None,) * len(lead)
    return (pl.BlockSpec(none + (tm, width), lambda i, *_: lead + (jnp.minimum(i, pt - 1), 0)),
            pl.BlockSpec(none + (tm, width), lambda i, *_: lead + (jnp.maximum(i - pt, 0), 0)))


def _combine_ple(dest, x1, route, yb, p_prompt, p_sample, g_p, w_plg, w_ple, g_final, layer, *, final, tm=256):
    n, d = x1.shape
    n_prompt, pd = p_prompt.shape[1:]
    assert n_prompt % tm == 0 and (n - n_prompt) % tm == 0
    pt = n_prompt // tm
    row = lambda width: pl.BlockSpec((tm, width), lambda i, dr: (i, 0))
    if final:
        out_specs = list(_split_rows(tm, d, pt))
        out_shape = [jax.ShapeDtypeStruct((n_prompt, d), F32), jax.ShapeDtypeStruct((n - n_prompt, d), F32)]
    else:
        out_specs = [row(d)]
        out_shape = [jax.ShapeDtypeStruct((n, d), F32)]
    grid_spec = pltpu.PrefetchScalarGridSpec(
        num_scalar_prefetch=1,
        grid=(n // tm,),
        in_specs=[row(d), row(LANE), pl.BlockSpec(memory_space=pl.ANY), *_split_rows(tm, pd, pt, (layer,)),
                  _layer_spec((1, d), layer), _layer_spec((d, d), layer), _layer_spec((pd, d), layer),
                  pl.BlockSpec((1, d), lambda i, dr: (0, 0))],
        out_specs=out_specs,
        scratch_shapes=[pltpu.VMEM((2, TOPK_IN_GROUP, tm, d), F32), pltpu.SemaphoreType.DMA((2,))],
    )
    return pl.pallas_call(
        functools.partial(_combine_ple_kernel, tm=tm, prompt_tiles=pt, final=final),
        grid_spec=grid_spec,
        out_shape=out_shape,
        compiler_params=_params(),
        name="combine_ple",
    )(dest, x1, route, yb, p_prompt, p_sample, g_p, w_plg, w_ple, g_final.reshape(1, d))


def kernel(x_prompt, x_sample, cache_k, cache_v, state_pool, page_table, p_prompt, p_sample, w_in, w_attn_out, w_pool_out, w_out, w_pool_group, pool_scale, g_mix, g_ffn, g_ple, g_final, w_router_group, b_router_group, w_router_expert, b_router_expert, w_exp_gate, w_exp_up, w_exp_down, w_ple, w_ple_gate):
    nb_, seq, d = x_prompt.shape
    n_seq, n_new, _ = x_sample.shape
    depth = w_in.shape[0]
    n_p = nb_ * seq
    n_s = n_seq * n_new
    past_len = page_table.shape[1] * PAGE_SIZE
    u_col = 3 * ATTN_W // POOL_W
    ga_col = (3 * ATTN_W + POOL_W) // d
    assert 3 * ATTN_W == u_col * POOL_W and 3 * ATTN_W + POOL_W == ga_col * d

    x = jnp.concatenate([x_prompt.reshape(n_p, d), x_sample.reshape(n_s, d)], axis=0)
    outs = {name: [] for name in ("pool_p", "ks", "vs", "pool_s")}
    pad_new = lambda t: jnp.pad(t.reshape(n_seq, n_new, ATTN_W), ((0, 0), (0, SUBLANE - n_new), (0, 0)))
    heads_s = lambda t: t.reshape(n_seq, n_new, N_HEADS, HEAD_DIM).transpose(0, 2, 1, 3)

    per_layer_row = lambda g: g.reshape(depth, 1, g.shape[-1])
    w_in_b, w_ao_b, w_po_b, w_o_b, w_grp_b, w_plg_b, w_ple_b = (
        w.astype(BF16) for w in (w_in, w_attn_out, w_pool_out, w_out, w_pool_group, w_ple_gate, w_ple))
    lane_pad = LANE - N_EXPERTS - N_EXPERT_GROUPS
    w_router = jnp.pad(jnp.concatenate([w_router_expert, w_router_group], axis=2), ((0, 0), (0, 0), (0, lane_pad)))
    b_router = jnp.pad(jnp.concatenate([b_router_expert, b_router_group], axis=1),
                       ((0, 0), (0, lane_pad))).reshape(depth, 1, LANE)
    pp, ps = p_prompt.reshape(depth, n_p, -1), p_sample.reshape(depth, n_s, -1)
    kv_p = None

    for i in range(depth):
        proj = _inproj(x, per_layer_row(g_mix), w_in_b, i)
        attn_p, k_all, v_all = _moba_prompt(proj, nb_, seq, i, depth, kv_p)
        kv_p = (k_all, v_all)
        q_s, k_s, v_s = (proj[n_p:, c * ATTN_W:(c + 1) * ATTN_W] for c in range(3))
        attn_s = _moba_sample(pad_new(q_s), pad_new(k_s), pad_new(v_s), cache_k, cache_v, page_table, i, n_new)
        attn_s = attn_s[:, :n_new].reshape(n_s, ATTN_W).astype(BF16)
        u_tail = jnp.stack([lax.slice(proj, ((b + 1) * seq - POOL_HIST, 3 * ATTN_W), ((b + 1) * seq, 3 * ATTN_W + POOL_W))
                            for b in range(nb_)])
        u_s = proj[n_p:, 3 * ATTN_W:3 * ATTN_W + POOL_W].reshape(n_seq, n_new, POOL_W)
        d_p = _pool_prompt(proj, nb_, seq, u_col)
        d_s = _pool_sample(state_pool[i].transpose(1, 0, 2), u_s.transpose(1, 0, 2), past_len)
        d_s = d_s.transpose(1, 0, 2).reshape(n_s, POOL_W)
        m = _merge(attn_p, attn_s, d_p, d_s, proj, ga_col, w_ao_b, w_grp_b, w_po_b, per_layer_row(pool_scale), i)
        x1, h, route, counts = _outproj_router(x, m, w_o_b, per_layer_row(g_ffn), w_router, b_router, i)
        dest, blk, e, lo, hi, first, nxt = _dispatch_tables(route, counts)
        xs = _dispatch(dest, h)
        yb = _moe_ffn(xs, blk, e, lo, hi, first, nxt, w_exp_gate, w_exp_up, w_exp_down, i)
        res = _combine_ple(dest, x1, route, yb, pp, ps, per_layer_row(g_ple), w_plg_b, w_ple_b, g_final, i,
                           final=(i == depth - 1))
        x = res[0]

        outs["pool_p"].append(u_tail)
        outs["ks"].append(heads_s(k_s))
        outs["vs"].append(heads_s(v_s))
        outs["pool_s"].append(jnp.concatenate([state_pool[i], u_s], axis=1)[:, n_new:])

    y_prompt = res[0].reshape(nb_, seq, d)
    y_sample = res[1].reshape(n_seq, n_new, d)
    stack = lambda name: jnp.stack(outs[name])
    return (y_prompt, y_sample, kv_p[0], kv_p[1], stack("pool_p"),
            stack("ks"), stack("vs"), stack("pool_s"))
```

```python
import functools

import jax
import jax.numpy as jnp
from jax import lax
from jax.experimental import pallas as pl
from jax.experimental.pallas import tpu as pltpu

F32 = jnp.float32
BF16 = jnp.bfloat16

LANE = 128
SUBLANE = 8
V7X_VMEM_LIMIT_BYTES = 56 * 1024 * 1024

N_HEADS = 8
HEAD_DIM = 128
ATTN_W = N_HEADS * HEAD_DIM
MOBA_BLOCK = 256
MOBA_TOPK = 3
PAGE_SIZE = 128
POOL_WINDOWS = (2, 4, 8, 16)
POOL_GW = 256
POOL_W = len(POOL_WINDOWS) * POOL_GW
POOL_HIST = max(POOL_WINDOWS) - 1
N_EXPERT_GROUPS = 4
EXPERTS_PER_GROUP = 8
N_EXPERTS = N_EXPERT_GROUPS * EXPERTS_PER_GROUP
TOPK_IN_GROUP = 2
MOE_BLOCK = 128
RMS_EPS = 1e-6
NEG = -1e30

_NT = (((1,), (1,)), ((), ()))
_TN = (((0,), (0,)), ((), ()))
_HI = lax.Precision.HIGHEST


def _params():
    return pltpu.CompilerParams(vmem_limit_bytes=V7X_VMEM_LIMIT_BYTES)


def _rmsnorm(x, g):
    return x * lax.rsqrt(jnp.mean(x * x, axis=-1, keepdims=True) + RMS_EPS) * g


def _inproj_kernel(x_ref, g_ref, w_ref, o_ref, xn_ref):
    @pl.when(pl.program_id(1) == 0)
    def _():
        xn_ref[...] = _rmsnorm(x_ref[...], g_ref[...]).astype(BF16)

    o_ref[...] = jnp.dot(xn_ref[...], w_ref[...], preferred_element_type=F32)


def _layer_spec(shape, layer):
    return pl.BlockSpec((None,) + tuple(shape), lambda *_: (layer,) + (0,) * len(shape))


def _row_tile(n, target):
    best = SUBLANE
    for t in range(SUBLANE, target + 1, SUBLANE):
        if n % t == 0:
            best = t
    return best


def _inproj(x, g_all, w_all_bf16, layer, *, tn=1024):
    n, d = x.shape
    pw = w_all_bf16.shape[2]
    tm = _row_tile(n, 1152)
    return pl.pallas_call(
        _inproj_kernel,
        grid=(n // tm, pw // tn),
        in_specs=[pl.BlockSpec((tm, d), lambda i, j: (i, 0)),
                  _layer_spec((1, d), layer),
                  pl.BlockSpec((None, d, tn), lambda i, j: (layer, 0, j))],
        out_specs=pl.BlockSpec((tm, tn), lambda i, j: (i, j)),
        out_shape=jax.ShapeDtypeStruct((n, pw), F32),
        scratch_shapes=[pltpu.VMEM((tm, d), BF16)],
        compiler_params=_params(),
        name="inproj",
    )(x, g_all, w_all_bf16)


def _moba_prompt_kernel(q_ref, k_ref, v_ref, o_ref, ko_ref, vo_ref, s_ref, *, nb):
    blk = MOBA_BLOCK
    scale = HEAD_DIM ** -0.5
    k_sel = min(MOBA_TOPK, nb - 1)
    k = k_ref[...]
    v = v_ref[...]
    ko_ref[...] = jnp.broadcast_to(k, ko_ref.shape)
    vo_ref[...] = jnp.broadcast_to(v, vo_ref.shape)
    kb = k.astype(BF16)
    vb = v.astype(BF16)
    km = jnp.concatenate([jnp.mean(k[j * blk:(j + 1) * blk], axis=0, keepdims=True) for j in range(nb)]
                         + [jnp.zeros((LANE - nb, HEAD_DIM), F32)], axis=0)
    km_hi = km.astype(BF16)
    km3 = jnp.concatenate([km_hi, km_hi, (km - km_hi.astype(F32)).astype(BF16)], axis=1)
    row = lax.broadcasted_iota(jnp.int32, (blk, blk), 0)
    col = lax.broadcasted_iota(jnp.int32, (blk, blk), 1)
    causal = col <= row
    for n in range(nb):
        q = q_ref[n * blk:(n + 1) * blk, :]
        qs = (q * scale).astype(BF16)
        sel = None
        if n > k_sel:
            q_hi = q.astype(BF16)
            q3 = jnp.concatenate([q_hi, (q - q_hi.astype(F32)).astype(BF16), q_hi], axis=1)
            gates = lax.dot_general(q3, km3, _NT, preferred_element_type=F32)
            g = [jnp.broadcast_to(gates[:, j:j + 1], (blk, LANE)) for j in range(n)]
            rank = [jnp.zeros((blk, LANE), F32) for _ in range(n)]
            for j in range(n):
                for jp in range(j):
                    first_wins = (g[jp] >= g[j]).astype(F32)
                    rank[j] = rank[j] + first_wins
                    rank[jp] = rank[jp] + (1.0 - first_wins)
            sel = [r < k_sel for r in rank]
        for j in range(n + 1):
            s = lax.dot_general(qs, kb[j * blk:(j + 1) * blk], _NT, preferred_element_type=F32)
            if j == n:
                s_ref[:, j * blk:(j + 1) * blk] = jnp.where(causal, s, NEG)
            elif sel is None:
                s_ref[:, j * blk:(j + 1) * blk] = s
            else:
                for half in range(blk // LANE):
                    lo = half * LANE
                    s_ref[:, j * blk + lo:j * blk + lo + LANE] = jnp.where(sel[j], s[:, lo:lo + LANE], NEG)
        nk = (n + 1) * blk
        s_all = s_ref[:, :nk]
        m = jnp.max(s_all, axis=-1, keepdims=True)
        p = jnp.exp(s_all - m)
        l = jnp.sum(p, axis=-1, keepdims=True)
        o = jnp.dot(p.astype(BF16), vb[:nk], preferred_element_type=F32) / l
        o_ref[n * blk:(n + 1) * blk, :] = o.astype(o_ref.dtype)


def _moba_prompt_kernel_aliased(q_ref, k_ref, v_ref, kprev_ref, vprev_ref, *rest, nb):
    del kprev_ref, vprev_ref
    _moba_prompt_kernel(q_ref, k_ref, v_ref, *rest, nb=nb)


def _moba_prompt(proj, n_batch, seq, layer, depth, kv_prev):
    nb = seq // MOBA_BLOCK
    assert nb * MOBA_BLOCK == seq
    blk_spec = lambda off: pl.BlockSpec((seq, HEAD_DIM), lambda b, h: (b, off + h))
    kv_shape = jax.ShapeDtypeStruct((depth, n_batch, N_HEADS, seq, HEAD_DIM), F32)
    in_specs = [blk_spec(0), blk_spec(N_HEADS), blk_spec(2 * N_HEADS)]
    args = [proj, proj, proj]
    body, aliases = _moba_prompt_kernel, {}
    if kv_prev is None:
        kv_out = pl.BlockSpec((depth, None, None, seq, HEAD_DIM), lambda b, h: (0, b, h, 0, 0))
    else:
        kv_out = pl.BlockSpec((None, None, None, seq, HEAD_DIM), lambda b, h: (layer, b, h, 0, 0))
        in_specs += [pl.BlockSpec(memory_space=pl.ANY)] * 2
        args += list(kv_prev)
        body, aliases = _moba_prompt_kernel_aliased, {3: 1, 4: 2}
    return pl.pallas_call(
        functools.partial(body, nb=nb),
        grid=(n_batch, N_HEADS),
        in_specs=in_specs,
        out_specs=[pl.BlockSpec((seq, HEAD_DIM), lambda b, h: (b, h)), kv_out, kv_out],
        out_shape=[jax.ShapeDtypeStruct((n_batch * seq, ATTN_W), BF16), kv_shape, kv_shape],
        scratch_shapes=[pltpu.VMEM((MOBA_BLOCK, seq), F32)],
        input_output_aliases=aliases,
        compiler_params=_params(),
        name="moba_prompt",
    )(*args)


def _moba_sample_kernel(pt_ref, q_ref, kn_ref, vn_ref, *refs, n_pages, n_new):
    del pt_ref
    k_refs = refs[:n_pages]
    v_refs = refs[n_pages:2 * n_pages]
    o_ref = refs[2 * n_pages]
    st_ref, vcat_ref, km_ref = refs[2 * n_pages + 1:]
    ps = PAGE_SIZE
    ppb = MOBA_BLOCK // ps
    nbp = n_pages // ppb
    k_sel = min(MOBA_TOPK, nbp)
    scale = HEAD_DIM ** -0.5

    q8 = q_ref[0]
    qrep = jnp.concatenate([q8] * (LANE // SUBLANE), axis=0)
    cc = lax.broadcasted_iota(jnp.int32, (LANE, ATTN_W), 0)
    ll = lax.broadcasted_iota(jnp.int32, (LANE, ATTN_W), 1)
    qrep = jnp.where(ll // HEAD_DIM == cc // SUBLANE, qrep, 0.0)
    qs_b = (qrep * scale).astype(BF16)

    acc = None
    for p in range(n_pages):
        kcat = jnp.concatenate([k_refs[p][h] for h in range(N_HEADS)], axis=1)
        vcat = jnp.concatenate([v_refs[p][h] for h in range(N_HEADS)], axis=1)
        st_ref[p * ps:(p + 1) * ps, :] = lax.dot_general(kcat.astype(BF16), qs_b, _NT,
                                                         preferred_element_type=F32)
        vcat_ref[p * ps:(p + 1) * ps, :] = vcat.astype(BF16)
        colsum = jnp.sum(kcat, axis=0, keepdims=True)
        acc = colsum if p % ppb == 0 else acc + colsum
        if p % ppb == ppb - 1:
            j = p // ppb
            km_ref[j:j + 1, :] = acc * (1.0 / MOBA_BLOCK)

    gate = lax.dot_general(km_ref[...], qrep, _NT, precision=_HI, preferred_element_type=F32)
    jidx = lax.broadcasted_iota(jnp.int32, (nbp, LANE), 0)
    sel = []
    for j in range(nbp):
        gj = gate[j:j + 1, :]
        beats = (gate > gj) | ((gate == gj) & (jidx < j))
        sel.append(jnp.sum(beats.astype(F32), axis=0, keepdims=True) < k_sel)

    kn8 = kn_ref[0]
    vn8 = vn_ref[0]
    sn = lax.dot_general(kn8.astype(BF16), qs_b, _NT, preferred_element_type=F32)
    s_i = lax.broadcasted_iota(jnp.int32, (SUBLANE, LANE), 0)
    t_c = lax.broadcasted_iota(jnp.int32, (SUBLANE, LANE), 1) % SUBLANE
    sn = jnp.where((s_i <= t_c) & (s_i < n_new), sn, NEG)
    m = jnp.max(sn, axis=0, keepdims=True)
    for j in range(nbp):
        mj = jnp.max(st_ref[j * MOBA_BLOCK:(j + 1) * MOBA_BLOCK, :], axis=0, keepdims=True)
        m = jnp.where(sel[j], jnp.maximum(m, mj), m)
    pn = jnp.exp(sn - m)
    l = jnp.sum(pn, axis=0, keepdims=True)
    for j in range(nbp):
        rows = slice(j * MOBA_BLOCK, (j + 1) * MOBA_BLOCK)
        pj = jnp.where(sel[j], jnp.exp(st_ref[rows, :] - m), 0.0)
        l = l + jnp.sum(pj, axis=0, keepdims=True)
        st_ref[rows, :] = pj
    inv = 1.0 / l
    pt = (st_ref[...] * inv).astype(BF16)
    o_full = lax.dot_general(pt, vcat_ref[...], _TN, preferred_element_type=F32)
    o_full = o_full + lax.dot_general((pn * inv).astype(BF16), vn8.astype(BF16), _TN,
                                      preferred_element_type=F32)
    o_ref[0] = jnp.concatenate(
        [o_full[h * SUBLANE:(h + 1) * SUBLANE, h * HEAD_DIM:(h + 1) * HEAD_DIM] for h in range(N_HEADS)], axis=1)


def _moba_sample(q8, kn8, vn8, cache_k, cache_v, page_table, layer, n_new):
    assert n_new <= SUBLANE
    n_seq, n_pages = page_table.shape
    assert (n_pages * PAGE_SIZE) % MOBA_BLOCK == 0
    tok_spec = pl.BlockSpec((1, SUBLANE, ATTN_W), lambda s, pt: (s, 0, 0))

    def page_spec(p):
        return pl.BlockSpec((None, None, N_HEADS, PAGE_SIZE, HEAD_DIM),
                            lambda s, pt: (layer, pt[s, p], 0, 0, 0))

    past = n_pages * PAGE_SIZE
    grid_spec = pltpu.PrefetchScalarGridSpec(
        num_scalar_prefetch=1,
        grid=(n_seq,),
        in_specs=[tok_spec, tok_spec, tok_spec] + [page_spec(p) for p in range(n_pages)] * 2,
        out_specs=tok_spec,
        scratch_shapes=[pltpu.VMEM((past, LANE), F32), pltpu.VMEM((past, ATTN_W), BF16),
                        pltpu.VMEM((past // MOBA_BLOCK, ATTN_W), F32)],
    )
    return pl.pallas_call(
        functools.partial(_moba_sample_kernel, n_pages=n_pages, n_new=n_new),
        grid_spec=grid_spec,
        out_shape=jax.ShapeDtypeStruct((n_seq, SUBLANE, ATTN_W), F32),
        compiler_params=_params(),
        name="moba_sample",
    )(page_table, q8, kn8, vn8, *([cache_k] * n_pages), *([cache_v] * n_pages))


def _pool_prompt_kernel(halo_ref, u_ref, d_ref, *, tile, halo):
    t = pl.program_id(1)
    u = u_ref[...]
    prev = jnp.where(t > 0, halo_ref[...], 0.0)
    ucat = jnp.concatenate([prev, u], axis=0)
    hi = ucat.astype(BF16)
    lo = (ucat - hi.astype(F32)).astype(BF16)
    r = lax.broadcasted_iota(jnp.int32, (tile, halo + tile), 0) + halo
    c = lax.broadcasted_iota(jnp.int32, (tile, halo + tile), 1)
    pos = t * tile + lax.broadcasted_iota(jnp.int32, (tile, 1), 0)
    for g, w in enumerate(POOL_WINDOWS):
        band = ((c <= r) & (c > r - w)).astype(BF16)
        sl = slice(g * POOL_GW, (g + 1) * POOL_GW)
        acc = (jnp.dot(band, hi[:, sl], preferred_element_type=F32)
               + jnp.dot(band, lo[:, sl], preferred_element_type=F32))
        cnt = jnp.minimum(pos + 1, w).astype(F32)
        d_ref[:, sl] = (acc / cnt - u[:, sl]).astype(d_ref.dtype)


def _pool_prompt(proj, n_batch, seq, u_col, *, tile=256, halo=128):
    assert halo >= POOL_HIST and tile % halo == 0 and seq % tile == 0
    tiles = seq // tile
    per = tile // halo
    return pl.pallas_call(
        functools.partial(_pool_prompt_kernel, tile=tile, halo=halo),
        grid=(n_batch, tiles),
        in_specs=[pl.BlockSpec((halo, POOL_W), lambda b, t: (jnp.maximum((b * tiles + t) * per - 1, 0), u_col)),
                  pl.BlockSpec((tile, POOL_W), lambda b, t: (b * tiles + t, u_col))],
        out_specs=pl.BlockSpec((tile, POOL_W), lambda b, t: (b * tiles + t, 0)),
        out_shape=jax.ShapeDtypeStruct((n_batch * seq, POOL_W), BF16),
        compiler_params=_params(),
        name="pool_prompt",
    )(proj, proj)


def _pool_sample_kernel(h_ref, u_ref, d_ref, *, n_hist, n_new, pos0):
    for g, w in enumerate(POOL_WINDOWS):
        sl = slice(g * POOL_GW, (g + 1) * POOL_GW)
        rows = [h_ref[j, :, sl] for j in range(n_hist)] + [u_ref[t, :, sl] for t in range(n_new)]
        for t in range(n_new):
            ti = n_hist + t
            first = max(ti + 1 - w, 0)
            acc = rows[first]
            for j in range(first + 1, ti + 1):
                acc = acc + rows[j]
            cnt = float(min(pos0 + t + 1, w))
            d_ref[t, :, sl] = (acc / cnt - rows[ti]).astype(d_ref.dtype)


def _pool_sample(hist_t, u_t, pos0, *, chunk=32):
    n_hist, n_seq, _ = hist_t.shape
    n_new = u_t.shape[0]
    return pl.pallas_call(
        functools.partial(_pool_sample_kernel, n_hist=n_hist, n_new=n_new, pos0=pos0),
        grid=(n_seq // chunk,),
        in_specs=[pl.BlockSpec((n_hist, chunk, POOL_W), lambda i: (0, i, 0)),
                  pl.BlockSpec((n_new, chunk, POOL_W), lambda i: (0, i, 0))],
        out_specs=pl.BlockSpec((n_new, chunk, POOL_W), lambda i: (0, i, 0)),
        out_shape=jax.ShapeDtypeStruct((n_new, n_seq, POOL_W), BF16),
        compiler_params=_params(),
        name="pool_sample",
    )(hist_t, u_t)


def _merge_kernel(ap_ref, as_ref, dp_ref, ds_ref, ga_ref, gb_ref, wao_ref, wgrp_ref, wpo_ref, sc_ref, m_ref, *,
                  prompt_tiles):
    is_prompt = pl.program_id(0) < prompt_tiles
    attn = jnp.where(is_prompt, ap_ref[...], as_ref[...])
    dd = jnp.where(is_prompt, dp_ref[...], ds_ref[...])
    a = jnp.dot(attn, wao_ref[...], preferred_element_type=F32)
    ys = [jnp.dot(dd[:, g * POOL_GW:(g + 1) * POOL_GW], wgrp_ref[g], preferred_element_type=F32)
          for g in range(len(POOL_WINDOWS))]
    pooled = (jnp.concatenate(ys, axis=1) * sc_ref[...]).astype(BF16)
    b = jnp.dot(pooled, wpo_ref[...], preferred_element_type=F32)
    m = jax.nn.sigmoid(ga_ref[...]) * a + jax.nn.sigmoid(gb_ref[...]) * b
    m_ref[...] = m.astype(m_ref.dtype)


def _merge(attn_p, attn_s, d_p, d_s, proj, ga_col, w_ao, w_grp, w_po, scale, layer, *, tm=256):
    n = proj.shape[0]
    dm = w_ao.shape[2]
    assert attn_p.shape[0] % tm == 0 and attn_s.shape[0] % tm == 0
    pt = attn_p.shape[0] // tm
    return pl.pallas_call(
        functools.partial(_merge_kernel, prompt_tiles=pt),
        grid=(n // tm,),
        in_specs=[*_split_rows(tm, ATTN_W, pt), *_split_rows(tm, POOL_W, pt),
                  pl.BlockSpec((tm, dm), lambda i: (i, ga_col)),
                  pl.BlockSpec((tm, dm), lambda i: (i, ga_col + 1)),
                  _layer_spec(w_ao.shape[1:], layer), _layer_spec(w_grp.shape[1:], layer),
                  _layer_spec(w_po.shape[1:], layer), _layer_spec((1, POOL_W), layer)],
        out_specs=pl.BlockSpec((tm, dm), lambda i: (i, 0)),
        out_shape=jax.ShapeDtypeStruct((n, dm), BF16),
        compiler_params=_params(),
        name="merge",
    )(attn_p, attn_s, d_p, d_s, proj, proj, w_ao, w_grp, w_po, scale)


def _first_lane_of_max(vals, lane):
    top = jnp.max(vals, axis=-1, keepdims=True)
    idx = jnp.min(jnp.where(vals == top, lane, LANE), axis=-1, keepdims=True)
    return top, idx


def _one_row(ref, row):
    return ref.at[pl.ds(row, 1)]


def _outproj_router_kernel(x_ref, m_ref, wo_ref, gf_ref, wr_ref, br_ref, x1_ref, h_ref, route_ref, cnt_ref,
                           run_ref, wr3_ref):
    step = pl.program_id(0)

    d = x_ref.shape[1]

    @pl.when(step == 0)
    def _():
        run_ref[...] = jnp.zeros_like(run_ref)
        w = wr_ref[...]
        w_hi = w.astype(BF16)
        wr3_ref[:, 0:LANE] = w_hi
        wr3_ref[:, LANE:2 * LANE] = (w - w_hi.astype(F32)).astype(BF16)

    x1 = x_ref[...] + jnp.dot(m_ref[...], wo_ref[...], preferred_element_type=F32)
    x1_ref[...] = x1
    h = _rmsnorm(x1, gf_ref[...])
    h_ref[...] = h
    h_hi = h.astype(BF16)
    h_lo = (h - h_hi.astype(F32)).astype(BF16)
    hi_both = jnp.dot(h_hi, wr3_ref[...], preferred_element_type=F32)
    logits = (hi_both[:, :LANE] + hi_both[:, LANE:]
              + jnp.dot(h_lo, wr3_ref[:, 0:LANE], preferred_element_type=F32) + br_ref[...])
    lane = lax.broadcasted_iota(jnp.int32, logits.shape, 1)
    is_g = (lane >= N_EXPERTS) & (lane < N_EXPERTS + N_EXPERT_GROUPS)
    gmax, gidx = _first_lane_of_max(jnp.where(is_g, logits, -jnp.inf), lane)
    p_g = 1.0 / jnp.sum(jnp.where(is_g, jnp.exp(logits - gmax), 0.0), axis=-1, keepdims=True)
    in_grp = (lane // EXPERTS_PER_GROUP == gidx - N_EXPERTS) & (lane < N_EXPERTS)
    e1 = jnp.where(in_grp, logits, -jnp.inf)
    t1, i1 = _first_lane_of_max(e1, lane)
    t2, i2 = _first_lane_of_max(jnp.where(lane == i1, -jnp.inf, e1), lane)
    z = jnp.exp(t2 - t1)
    w1 = p_g / (1.0 + z)
    w2 = p_g * (z / (1.0 + z))
    oh1 = lane == i1
    oh2 = lane == i2
    picks = (oh1 | oh2).astype(F32)
    tm = picks.shape[0]
    earlier = (lax.broadcasted_iota(jnp.int32, (tm, tm), 1)
               < lax.broadcasted_iota(jnp.int32, (tm, tm), 0)).astype(BF16)
    before = jnp.dot(earlier, picks.astype(BF16), preferred_element_type=F32) + run_ref[...]
    r1 = jnp.sum(jnp.where(oh1, before, 0.0), axis=-1, keepdims=True)
    r2 = jnp.sum(jnp.where(oh2, before, 0.0), axis=-1, keepdims=True)
    run_ref[...] = run_ref[...] + jnp.sum(picks, axis=0, keepdims=True)
    cnt_ref[...] = jnp.broadcast_to(run_ref[...], cnt_ref.shape)
    route = jnp.zeros(logits.shape, F32)
    for k, val in enumerate((i1.astype(F32), i2.astype(F32), w1, w2, r1, r2)):
        route = jnp.where(lane == k, val, route)
    route_ref[...] = route


def _outproj_router(x, m, w_o, g_f, w_router, b_router, layer, *, tm=256):
    n, d = x.shape
    row = lambda width: pl.BlockSpec((tm, width), lambda i: (i, 0))
    return pl.pallas_call(
        _outproj_router_kernel,
        grid=(n // tm,),
        in_specs=[row(d), row(d), _layer_spec((d, d), layer), _layer_spec((1, d), layer),
                  _layer_spec((d, LANE), layer), _layer_spec((1, LANE), layer)],
        out_specs=[row(d), row(d), row(LANE), pl.BlockSpec((SUBLANE, LANE), lambda i: (0, 0))],
        out_shape=[jax.ShapeDtypeStruct((n, d), F32), jax.ShapeDtypeStruct((n, d), F32),
                   jax.ShapeDtypeStruct((n, LANE), F32), jax.ShapeDtypeStruct((SUBLANE, LANE), F32)],
        scratch_shapes=[pltpu.VMEM((1, LANE), F32), pltpu.VMEM((d, 2 * LANE), BF16)],
        compiler_params=_params(),
        name="outproj_router",
    )(x, m, w_o, g_f, w_router, b_router)


def _dispatch_tables(route, counts_f):
    n = route.shape[0]
    a = n * TOPK_IN_GROUP
    n_blocks = a // MOE_BLOCK
    assert n_blocks * MOE_BLOCK == a
    n_items = n_blocks + N_EXPERTS - 1
    counts = counts_f[0, :N_EXPERTS].astype(jnp.int32)
    end = jnp.cumsum(counts)
    start = end - counts
    eid = route[:, :TOPK_IN_GROUP].astype(jnp.int32)
    rank = route[:, 2 * TOPK_IN_GROUP:3 * TOPK_IN_GROUP].astype(jnp.int32)
    onehot = eid[:, :, None] == jnp.arange(N_EXPERTS, dtype=jnp.int32)[None, None, :]
    dest = (jnp.sum(jnp.where(onehot, start[None, None, :], 0), axis=-1) + rank).reshape(a)

    first_blk = start // MOE_BLOCK
    items_e = jnp.where(counts > 0, (end - 1) // MOE_BLOCK - first_blk + 1, 0)
    item_end = jnp.cumsum(items_e)
    item_off = item_end - items_e
    total = item_end[-1]
    idx = jnp.arange(n_items, dtype=jnp.int32)
    e = jnp.minimum(jnp.sum((idx[:, None] >= item_end[None, :]).astype(jnp.int32), axis=1), N_EXPERTS - 1)
    pick = lambda tbl, ee: jnp.sum(jnp.where(ee[:, None] == jnp.arange(N_EXPERTS)[None, :], tbl[None, :], 0), axis=1)
    blk = pick(first_blk, e) + idx - pick(item_off, e)
    lo = jnp.maximum(pick(start, e), blk * MOE_BLOCK) - blk * MOE_BLOCK
    hi = jnp.minimum(pick(end, e), (blk + 1) * MOE_BLOCK) - blk * MOE_BLOCK
    valid = idx < total
    e_last = jnp.sum(jnp.where(idx == total - 1, e, 0))
    e = jnp.where(valid, e, e_last)
    blk = jnp.where(valid, blk, n_blocks - 1)
    lo = jnp.where(valid, lo, 0)
    hi = jnp.where(valid, hi, 0)
    first = jnp.concatenate([jnp.ones((1,), jnp.int32), (blk[1:] != blk[:-1]).astype(jnp.int32)])
    ids = jnp.arange(N_EXPERTS, dtype=jnp.int32)
    later = (ids[None, :] > ids[:, None]) & (counts[None, :] > 0)
    nxt_e = jnp.min(jnp.where(later, ids[None, :], N_EXPERTS), axis=1)
    nxt = pick(jnp.where(nxt_e < N_EXPERTS, nxt_e, -1), e)
    i32 = lambda t: t.astype(jnp.int32)
    return i32(dest), i32(blk), i32(e), i32(lo), i32(hi), first, i32(nxt)


def _dispatch_kernel(dest_ref, h_ref, xs_hbm, sem, *, tm):
    base = pl.program_id(0) * tm * TOPK_IN_GROUP

    def start(g, carry):
        r0 = pl.multiple_of(g * SUBLANE, SUBLANE)
        for u in range(SUBLANE):
            for k in range(TOPK_IN_GROUP):
                slot = dest_ref[base + (r0 + u) * TOPK_IN_GROUP + k]
                pltpu.make_async_copy(_one_row(h_ref, r0 + u), _one_row(xs_hbm, slot), sem).start(priority=k)
        return carry
    lax.fori_loop(0, tm // SUBLANE, start, 0)

    def wait(r, carry):
        for k in range(TOPK_IN_GROUP):
            pltpu.make_async_copy(_one_row(h_ref, 0), _one_row(xs_hbm, 0), sem).wait()
        return carry
    lax.fori_loop(0, tm, wait, 0, unroll=8)


def _dispatch(dest, h, *, tm=512):
    n, d = h.shape
    grid_spec = pltpu.PrefetchScalarGridSpec(
        num_scalar_prefetch=1,
        grid=(n // tm,),
        in_specs=[pl.BlockSpec((tm, d), lambda i, dr: (i, 0))],
        out_specs=pl.BlockSpec(memory_space=pl.ANY),
        scratch_shapes=[pltpu.SemaphoreType.DMA(())],
    )
    return pl.pallas_call(
        functools.partial(_dispatch_kernel, tm=tm),
        grid_spec=grid_spec,
        out_shape=jax.ShapeDtypeStruct((n * TOPK_IN_GROUP, d), F32),
        compiler_params=_params(),
        name="moe_dispatch",
    )(dest, h)


def _moe_ffn_kernel(blk_ref, e_ref, lo_ref, hi_ref, first_ref, nxt_ref, x_ref, wg_hbm, wu_hbm, wd_hbm, y_ref,
                    stage_g, stage_u, stage_d, wg_b, wu_b, wd_b, sems, cur_ref, *, layer):
    del blk_ref
    j = pl.program_id(0)

    def weight_copies(expert, s):
        return (pltpu.make_async_copy(wg_hbm.at[layer, expert], stage_g.at[s], sems.at[s]),
                pltpu.make_async_copy(wu_hbm.at[layer, expert], stage_u.at[s], sems.at[s]),
                pltpu.make_async_copy(wd_hbm.at[layer, expert], stage_d.at[s], sems.at[s]))

    @pl.when(j == 0)
    def _():
        cur_ref[0] = 1
        for c in weight_copies(e_ref[0], 0):
            c.start(priority=1)

    @pl.when((j == 0) | (e_ref[j] != e_ref[jnp.maximum(j - 1, 0)]))
    def _():
        s = 1 - cur_ref[0]
        cur_ref[0] = s

        @pl.when(nxt_ref[j] >= 0)
        def _():
            for c in weight_copies(nxt_ref[j], 1 - s):
                c.start(priority=1)

        for c in weight_copies(e_ref[j], s):
            c.wait()
        wg_b[...] = stage_g[s].astype(BF16)
        wu_b[...] = stage_u[s].astype(BF16)
        wd_b[...] = stage_d[s].astype(BF16)

    @pl.when(hi_ref[j] > lo_ref[j])
    def _():
        x = x_ref[...].astype(BF16)
        gate = jnp.dot(x, wg_b[...], preferred_element_type=F32)
        up = jnp.dot(x, wu_b[...], preferred_element_type=F32)
        act = (jax.nn.silu(gate) * up).astype(BF16)
        y = jnp.dot(act, wd_b[...], preferred_element_type=F32)
        row = lax.broadcasted_iota(jnp.int32, (MOE_BLOCK, 1), 0)
        y = jnp.where((row >= lo_ref[j]) & (row < hi_ref[j]), y, 0.0)

        @pl.when(first_ref[j] == 1)
        def _():
            y_ref[...] = y

        @pl.when(first_ref[j] == 0)
        def _():
            y_ref[...] = y_ref[...] + y


def _moe_ffn(xs, blk, e, lo, hi, first, nxt, w_gate, w_up, w_down, layer):
    a, d = xs.shape
    de = w_gate.shape[3]
    n_items = blk.shape[0]
    slab = pl.BlockSpec((MOE_BLOCK, d), lambda j, blk, *_: (blk[j], 0))
    hbm = pl.BlockSpec(memory_space=pl.ANY)
    grid_spec = pltpu.PrefetchScalarGridSpec(
        num_scalar_prefetch=6,
        grid=(n_items,),
        in_specs=[slab, hbm, hbm, hbm],
        out_specs=slab,
        scratch_shapes=[pltpu.VMEM((2, d, de), F32), pltpu.VMEM((2, d, de), F32), pltpu.VMEM((2, de, d), F32),
                        pltpu.VMEM((d, de), BF16), pltpu.VMEM((d, de), BF16), pltpu.VMEM((de, d), BF16),
                        pltpu.SemaphoreType.DMA((2,)), pltpu.SMEM((1,), jnp.int32)],
    )
    return pl.pallas_call(
        functools.partial(_moe_ffn_kernel, layer=layer),
        grid_spec=grid_spec,
        out_shape=jax.ShapeDtypeStruct((a, d), F32),
        compiler_params=_params(),
        name="moe_ffn",
    )(blk, e, lo, hi, first, nxt, xs, w_gate, w_up, w_down)


def _combine_ple_kernel(dest_ref, x_ref, route_ref, yb_hbm, pp_ref, ps_ref, gp_ref, wplg_ref, wple_ref, gfin_ref,
                        *rest, tm, prompt_tiles, final):
    outs, (ybuf, sems) = rest[:-2], rest[-2:]
    n_slots = ybuf.shape[0]
    i = pl.program_id(0)
    nt = pl.num_programs(0)
    slot = i % n_slots

    def start_gather(tile, s):
        def body(g, carry):
            r0 = pl.multiple_of(g * SUBLANE, SUBLANE)
            for u in range(SUBLANE):
                for k in range(TOPK_IN_GROUP):
                    src = dest_ref[(tile * tm + r0 + u) * TOPK_IN_GROUP + k]
                    pltpu.make_async_copy(_one_row(yb_hbm, src), _one_row(ybuf.at[s, k], r0 + u),
                                          sems.at[s]).start(priority=k)
            return carry
        lax.fori_loop(0, tm // SUBLANE, body, 0)

    @pl.when(i == 0)
    def _():
        for ahead in range(n_slots - 1):
            @pl.when(ahead < nt)
            def _():
                start_gather(ahead, ahead)

    @pl.when(i + n_slots - 1 < nt)
    def _():
        start_gather(i + n_slots - 1, (i + n_slots - 1) % n_slots)

    def wait_body(r, carry):
        for k in range(TOPK_IN_GROUP):
            pltpu.make_async_copy(_one_row(yb_hbm, 0), _one_row(ybuf.at[slot, k], 0), sems.at[slot]).wait()
        return carry
    lax.fori_loop(0, tm, wait_body, 0, unroll=8)

    route = route_ref[...]
    moe = sum(ybuf[slot, k] * route[:, TOPK_IN_GROUP + k:TOPK_IN_GROUP + k + 1] for k in range(TOPK_IN_GROUP))
    x2 = x_ref[...] + moe
    hp = _rmsnorm(x2, gp_ref[...]).astype(BF16)
    gate = jax.nn.sigmoid(jnp.dot(hp, wplg_ref[...], preferred_element_type=F32))
    p = jnp.where(i < prompt_tiles, pp_ref[...], ps_ref[...])
    pe = jnp.dot(p.astype(BF16), wple_ref[...], preferred_element_type=F32)
    x3 = x2 + gate * pe
    if not final:
        outs[0][...] = x3
    else:
        y = _rmsnorm(x3, gfin_ref[...])

        @pl.when(i < prompt_tiles)
        def _():
            outs[0][...] = y

        @pl.when(i >= prompt_tiles)
        def _():
            outs[1][...] = y


def _split_rows(tm, width, pt, lead=()):
    none = (None,) * len(lead)
    return (pl.BlockSpec(none + (tm, width), lambda i, *_: lead + (jnp.minimum(i, pt - 1), 0)),
            pl.BlockSpec(none + (tm, width), lambda i, *_: lead + (jnp.maximum(i - pt, 0), 0)))


def _combine_ple(dest, x1, route, yb, p_prompt, p_sample, g_p, w_plg, w_ple, g_final, layer, *, final, tm=256):
    n, d = x1.shape
    n_prompt, pd = p_prompt.shape[1:]
    assert n_prompt % tm == 0 and (n - n_prompt) % tm == 0
    pt = n_prompt // tm
    row = lambda width: pl.BlockSpec((tm, width), lambda i, dr: (i, 0))
    if final:
        out_specs = list(_split_rows(tm, d, pt))
        out_shape = [jax.ShapeDtypeStruct((n_prompt, d), F32), jax.ShapeDtypeStruct((n - n_prompt, d), F32)]
    else:
        out_specs = [row(d)]
        out_shape = [jax.ShapeDtypeStruct((n, d), F32)]
    grid_spec = pltpu.PrefetchScalarGridSpec(
        num_scalar_prefetch=1,
        grid=(n // tm,),
        in_specs=[row(d), row(LANE), pl.BlockSpec(memory_space=pl.ANY), *_split_rows(tm, pd, pt, (layer,)),
                  _layer_spec((1, d), layer), _layer_spec((d, d), layer), _layer_spec((pd, d), layer),
                  pl.BlockSpec((1, d), lambda i, dr: (0, 0))],
        out_specs=out_specs,
        scratch_shapes=[pltpu.VMEM((3, TOPK_IN_GROUP, tm, d), F32), pltpu.SemaphoreType.DMA((3,))],
    )
    return pl.pallas_call(
        functools.partial(_combine_ple_kernel, tm=tm, prompt_tiles=pt, final=final),
        grid_spec=grid_spec,
        out_shape=out_shape,
        compiler_params=_params(),
        name="combine_ple",
    )(dest, x1, route, yb, p_prompt, p_sample, g_p, w_plg, w_ple, g_final.reshape(1, d))


def kernel(x_prompt, x_sample, cache_k, cache_v, state_pool, page_table, p_prompt, p_sample, w_in, w_attn_out, w_pool_out, w_out, w_pool_group, pool_scale, g_mix, g_ffn, g_ple, g_final, w_router_group, b_router_group, w_router_expert, b_router_expert, w_exp_gate, w_exp_up, w_exp_down, w_ple, w_ple_gate):
    nb_, seq, d = x_prompt.shape
    n_seq, n_new, _ = x_sample.shape
    depth = w_in.shape[0]
    n_p = nb_ * seq
    n_s = n_seq * n_new
    past_len = page_table.shape[1] * PAGE_SIZE
    u_col = 3 * ATTN_W // POOL_W
    ga_col = (3 * ATTN_W + POOL_W) // d
    assert 3 * ATTN_W == u_col * POOL_W and 3 * ATTN_W + POOL_W == ga_col * d

    x = jnp.concatenate([x_prompt.reshape(n_p, d), x_sample.reshape(n_s, d)], axis=0)
    outs = {name: [] for name in ("pool_p", "ks", "vs", "pool_s")}
    pad_new = lambda t: jnp.pad(t.reshape(n_seq, n_new, ATTN_W), ((0, 0), (0, SUBLANE - n_new), (0, 0)))
    heads_s = lambda t: t.reshape(n_seq, n_new, N_HEADS, HEAD_DIM).transpose(0, 2, 1, 3)

    per_layer_row = lambda g: g.reshape(depth, 1, g.shape[-1])
    w_in_b, w_ao_b, w_po_b, w_o_b, w_grp_b, w_plg_b, w_ple_b = (
        w.astype(BF16) for w in (w_in, w_attn_out, w_pool_out, w_out, w_pool_group, w_ple_gate, w_ple))
    lane_pad = LANE - N_EXPERTS - N_EXPERT_GROUPS
    w_router = jnp.pad(jnp.concatenate([w_router_expert, w_router_group], axis=2), ((0, 0), (0, 0), (0, lane_pad)))
    b_router = jnp.pad(jnp.concatenate([b_router_expert, b_router_group], axis=1),
                       ((0, 0), (0, lane_pad))).reshape(depth, 1, LANE)
    pp, ps = p_prompt.reshape(depth, n_p, -1), p_sample.reshape(depth, n_s, -1)
    kv_p = None

    for i in range(depth):
        proj = _inproj(x, per_layer_row(g_mix), w_in_b, i)
        attn_p, k_all, v_all = _moba_prompt(proj, nb_, seq, i, depth, kv_p)
        kv_p = (k_all, v_all)
        q_s, k_s, v_s = (proj[n_p:, c * ATTN_W:(c + 1) * ATTN_W] for c in range(3))
        attn_s = _moba_sample(pad_new(q_s), pad_new(k_s), pad_new(v_s), cache_k, cache_v, page_table, i, n_new)
        attn_s = attn_s[:, :n_new].reshape(n_s, ATTN_W).astype(BF16)
        u_tail = jnp.stack([lax.slice(proj, ((b + 1) * seq - POOL_HIST, 3 * ATTN_W), ((b + 1) * seq, 3 * ATTN_W + POOL_W))
                            for b in range(nb_)])
        u_s = proj[n_p:, 3 * ATTN_W:3 * ATTN_W + POOL_W].reshape(n_seq, n_new, POOL_W)
        d_p = _pool_prompt(proj, nb_, seq, u_col)
        d_s = _pool_sample(state_pool[i].transpose(1, 0, 2), u_s.transpose(1, 0, 2), past_len)
        d_s = d_s.transpose(1, 0, 2).reshape(n_s, POOL_W)
        m = _merge(attn_p, attn_s, d_p, d_s, proj, ga_col, w_ao_b, w_grp_b, w_po_b, per_layer_row(pool_scale), i)
        x1, h, route, counts = _outproj_router(x, m, w_o_b, per_layer_row(g_ffn), w_router, b_router, i)
        dest, blk, e, lo, hi, first, nxt = _dispatch_tables(route, counts)
        xs = _dispatch(dest, h)
        yb = _moe_ffn(xs, blk, e, lo, hi, first, nxt, w_exp_gate, w_exp_up, w_exp_down, i)
        res = _combine_ple(dest, x1, route, yb, pp, ps, per_layer_row(g_ple), w_plg_b, w_ple_b, g_final, i,
                           final=(i == depth - 1))
        x = res[0]

        outs["pool_p"].append(u_tail)
        outs["ks"].append(heads_s(k_s))
        outs["vs"].append(heads_s(v_s))
        outs["pool_s"].append(jnp.concatenate([state_pool[i], u_s], axis=1)[:, n_new:])

    y_prompt = res[0].reshape(nb_, seq, d)
    y_sample = res[1].reshape(n_seq, n_new, d)
    stack = lambda name: jnp.stack(outs[name])
    return (y_prompt, y_sample, kv_p[0], kv_p[1], stack("pool_p"),
            stack("ks"), stack("vs"), stack("pool_s"))
```

```python
import functools

import jax
import jax.numpy as jnp
from jax import lax
from jax.experimental import pallas as pl
from jax.experimental.pallas import tpu as pltpu

F32 = jnp.float32
BF16 = jnp.bfloat16

LANE = 128
SUBLANE = 8
V7X_VMEM_LIMIT_BYTES = 56 * 1024 * 1024

N_HEADS = 8
HEAD_DIM = 128
ATTN_W = N_HEADS * HEAD_DIM
MOBA_BLOCK = 256
MOBA_TOPK = 3
PAGE_SIZE = 128
POOL_WINDOWS = (2, 4, 8, 16)
POOL_GW = 256
POOL_W = len(POOL_WINDOWS) * POOL_GW
POOL_HIST = max(POOL_WINDOWS) - 1
N_EXPERT_GROUPS = 4
EXPERTS_PER_GROUP = 8
N_EXPERTS = N_EXPERT_GROUPS * EXPERTS_PER_GROUP
TOPK_IN_GROUP = 2
MOE_BLOCK = 128
RMS_EPS = 1e-6
NEG = -1e30

_NT = (((1,), (1,)), ((), ()))
_TN = (((0,), (0,)), ((), ()))
_HI = lax.Precision.HIGHEST


def _params():
    return pltpu.CompilerParams(vmem_limit_bytes=V7X_VMEM_LIMIT_BYTES)


def _rmsnorm(x, g):
    return x * lax.rsqrt(jnp.mean(x * x, axis=-1, keepdims=True) + RMS_EPS) * g


def _inproj_kernel(x_ref, g_ref, w_ref, o_ref, xn_ref):
    @pl.when(pl.program_id(1) == 0)
    def _():
        xn_ref[...] = _rmsnorm(x_ref[...], g_ref[...]).astype(BF16)

    o_ref[...] = jnp.dot(xn_ref[...], w_ref[...], preferred_element_type=F32)


def _layer_spec(shape, layer):
    return pl.BlockSpec((None,) + tuple(shape), lambda *_: (layer,) + (0,) * len(shape),
                        pipeline_mode=pl.Buffered(1))


def _row_tile(n, target):
    best = SUBLANE
    for t in range(SUBLANE, target + 1, SUBLANE):
        if n % t == 0:
            best = t
    return best


def _inproj(x, g_all, w_all_bf16, layer, *, tn=1024):
    n, d = x.shape
    pw = w_all_bf16.shape[2]
    tm = _row_tile(n, 1152)
    return pl.pallas_call(
        _inproj_kernel,
        grid=(n // tm, pw // tn),
        in_specs=[pl.BlockSpec((tm, d), lambda i, j: (i, 0)),
                  _layer_spec((1, d), layer),
                  pl.BlockSpec((None, d, tn), lambda i, j: (layer, 0, j))],
        out_specs=pl.BlockSpec((tm, tn), lambda i, j: (i, j)),
        out_shape=jax.ShapeDtypeStruct((n, pw), F32),
        scratch_shapes=[pltpu.VMEM((tm, d), BF16)],
        compiler_params=_params(),
        name="inproj",
    )(x, g_all, w_all_bf16)


def _moba_prompt_kernel(q_ref, k_ref, v_ref, o_ref, ko_ref, vo_ref, s_ref, *, nb):
    blk = MOBA_BLOCK
    scale = HEAD_DIM ** -0.5
    k_sel = min(MOBA_TOPK, nb - 1)
    k = k_ref[...]
    v = v_ref[...]
    ko_ref[...] = jnp.broadcast_to(k, ko_ref.shape)
    vo_ref[...] = jnp.broadcast_to(v, vo_ref.shape)
    kb = k.astype(BF16)
    vb = v.astype(BF16)
    km = jnp.concatenate([jnp.mean(k[j * blk:(j + 1) * blk], axis=0, keepdims=True) for j in range(nb)]
                         + [jnp.zeros((LANE - nb, HEAD_DIM), F32)], axis=0)
    km_hi = km.astype(BF16)
    km3 = jnp.concatenate([km_hi, km_hi, (km - km_hi.astype(F32)).astype(BF16)], axis=1)
    row = lax.broadcasted_iota(jnp.int32, (blk, blk), 0)
    col = lax.broadcasted_iota(jnp.int32, (blk, blk), 1)
    causal = col <= row
    for n in range(nb):
        q = q_ref[n * blk:(n + 1) * blk, :]
        qs = (q * scale).astype(BF16)
        sel = None
        if n > k_sel:
            q_hi = q.astype(BF16)
            q3 = jnp.concatenate([q_hi, (q - q_hi.astype(F32)).astype(BF16), q_hi], axis=1)
            gates = lax.dot_general(q3, km3, _NT, preferred_element_type=F32)
            g = [jnp.broadcast_to(gates[:, j:j + 1], (blk, LANE)) for j in range(n)]
            rank = [jnp.zeros((blk, LANE), F32) for _ in range(n)]
            for j in range(n):
                for jp in range(j):
                    first_wins = (g[jp] >= g[j]).astype(F32)
                    rank[j] = rank[j] + first_wins
                    rank[jp] = rank[jp] + (1.0 - first_wins)
            sel = [r < k_sel for r in rank]
        for j in range(n + 1):
            s = lax.dot_general(qs, kb[j * blk:(j + 1) * blk], _NT, preferred_element_type=F32)
            if j == n:
                s_ref[:, j * blk:(j + 1) * blk] = jnp.where(causal, s, NEG)
            elif sel is None:
                s_ref[:, j * blk:(j + 1) * blk] = s
            else:
                for half in range(blk // LANE):
                    lo = half * LANE
                    s_ref[:, j * blk + lo:j * blk + lo + LANE] = jnp.where(sel[j], s[:, lo:lo + LANE], NEG)
        nk = (n + 1) * blk
        s_all = s_ref[:, :nk]
        m = jnp.max(s_all, axis=-1, keepdims=True)
        p = jnp.exp(s_all - m)
        l = jnp.sum(p, axis=-1, keepdims=True)
        o = jnp.dot(p.astype(BF16), vb[:nk], preferred_element_type=F32) / l
        o_ref[n * blk:(n + 1) * blk, :] = o.astype(o_ref.dtype)


def _moba_prompt_kernel_aliased(q_ref, k_ref, v_ref, kprev_ref, vprev_ref, *rest, nb):
    del kprev_ref, vprev_ref
    _moba_prompt_kernel(q_ref, k_ref, v_ref, *rest, nb=nb)


def _moba_prompt(proj, n_batch, seq, layer, depth, kv_prev):
    nb = seq // MOBA_BLOCK
    assert nb * MOBA_BLOCK == seq
    blk_spec = lambda off: pl.BlockSpec((seq, HEAD_DIM), lambda b, h: (b, off + h))
    kv_shape = jax.ShapeDtypeStruct((depth, n_batch, N_HEADS, seq, HEAD_DIM), F32)
    in_specs = [blk_spec(0), blk_spec(N_HEADS), blk_spec(2 * N_HEADS)]
    args = [proj, proj, proj]
    body, aliases = _moba_prompt_kernel, {}
    if kv_prev is None:
        kv_out = pl.BlockSpec((depth, None, None, seq, HEAD_DIM), lambda b, h: (0, b, h, 0, 0))
    else:
        kv_out = pl.BlockSpec((None, None, None, seq, HEAD_DIM), lambda b, h: (layer, b, h, 0, 0))
        in_specs += [pl.BlockSpec(memory_space=pl.ANY)] * 2
        args += list(kv_prev)
        body, aliases = _moba_prompt_kernel_aliased, {3: 1, 4: 2}
    return pl.pallas_call(
        functools.partial(body, nb=nb),
        grid=(n_batch, N_HEADS),
        in_specs=in_specs,
        out_specs=[pl.BlockSpec((seq, HEAD_DIM), lambda b, h: (b, h)), kv_out, kv_out],
        out_shape=[jax.ShapeDtypeStruct((n_batch * seq, ATTN_W), BF16), kv_shape, kv_shape],
        scratch_shapes=[pltpu.VMEM((MOBA_BLOCK, seq), F32)],
        input_output_aliases=aliases,
        compiler_params=_params(),
        name="moba_prompt",
    )(*args)


def _moba_sample_kernel(pt_ref, q_ref, kn_ref, vn_ref, *refs, n_pages, n_new):
    del pt_ref
    k_refs = refs[:n_pages]
    v_refs = refs[n_pages:2 * n_pages]
    o_ref = refs[2 * n_pages]
    st_ref, vcat_ref, km_ref = refs[2 * n_pages + 1:]
    ps = PAGE_SIZE
    ppb = MOBA_BLOCK // ps
    nbp = n_pages // ppb
    k_sel = min(MOBA_TOPK, nbp)
    scale = HEAD_DIM ** -0.5

    q8 = q_ref[0]
    qrep = jnp.concatenate([q8] * (LANE // SUBLANE), axis=0)
    cc = lax.broadcasted_iota(jnp.int32, (LANE, ATTN_W), 0)
    ll = lax.broadcasted_iota(jnp.int32, (LANE, ATTN_W), 1)
    qrep = jnp.where(ll // HEAD_DIM == cc // SUBLANE, qrep, 0.0)
    qs_b = (qrep * scale).astype(BF16)

    acc = None
    for p in range(n_pages):
        kcat = jnp.concatenate([k_refs[p][h] for h in range(N_HEADS)], axis=1)
        vcat = jnp.concatenate([v_refs[p][h] for h in range(N_HEADS)], axis=1)
        st_ref[p * ps:(p + 1) * ps, :] = lax.dot_general(kcat.astype(BF16), qs_b, _NT,
                                                         preferred_element_type=F32)
        vcat_ref[p * ps:(p + 1) * ps, :] = vcat.astype(BF16)
        colsum = jnp.sum(kcat, axis=0, keepdims=True)
        acc = colsum if p % ppb == 0 else acc + colsum
        if p % ppb == ppb - 1:
            j = p // ppb
            km_ref[j:j + 1, :] = acc * (1.0 / MOBA_BLOCK)

    gate = lax.dot_general(km_ref[...], qrep, _NT, precision=_HI, preferred_element_type=F32)
    jidx = lax.broadcasted_iota(jnp.int32, (nbp, LANE), 0)
    sel = []
    for j in range(nbp):
        gj = gate[j:j + 1, :]
        beats = (gate > gj) | ((gate == gj) & (jidx < j))
        sel.append(jnp.sum(beats.astype(F32), axis=0, keepdims=True) < k_sel)

    kn8 = kn_ref[0]
    vn8 = vn_ref[0]
    sn = lax.dot_general(kn8.astype(BF16), qs_b, _NT, preferred_element_type=F32)
    s_i = lax.broadcasted_iota(jnp.int32, (SUBLANE, LANE), 0)
    t_c = lax.broadcasted_iota(jnp.int32, (SUBLANE, LANE), 1) % SUBLANE
    sn = jnp.where((s_i <= t_c) & (s_i < n_new), sn, NEG)
    m = jnp.max(sn, axis=0, keepdims=True)
    for j in range(nbp):
        mj = jnp.max(st_ref[j * MOBA_BLOCK:(j + 1) * MOBA_BLOCK, :], axis=0, keepdims=True)
        m = jnp.where(sel[j], jnp.maximum(m, mj), m)
    pn = jnp.exp(sn - m)
    l = jnp.sum(pn, axis=0, keepdims=True)
    for j in range(nbp):
        rows = slice(j * MOBA_BLOCK, (j + 1) * MOBA_BLOCK)
        pj = jnp.where(sel[j], jnp.exp(st_ref[rows, :] - m), 0.0)
        l = l + jnp.sum(pj, axis=0, keepdims=True)
        st_ref[rows, :] = pj
    inv = 1.0 / l
    pt = (st_ref[...] * inv).astype(BF16)
    o_full = lax.dot_general(pt, vcat_ref[...], _TN, preferred_element_type=F32)
    o_full = o_full + lax.dot_general((pn * inv).astype(BF16), vn8.astype(BF16), _TN,
                                      preferred_element_type=F32)
    o_ref[0] = jnp.concatenate(
        [o_full[h * SUBLANE:(h + 1) * SUBLANE, h * HEAD_DIM:(h + 1) * HEAD_DIM] for h in range(N_HEADS)], axis=1)


def _moba_sample(q8, kn8, vn8, cache_k, cache_v, page_table, layer, n_new):
    assert n_new <= SUBLANE
    n_seq, n_pages = page_table.shape
    assert (n_pages * PAGE_SIZE) % MOBA_BLOCK == 0
    tok_spec = pl.BlockSpec((1, SUBLANE, ATTN_W), lambda s, pt: (s, 0, 0))

    def page_spec(p):
        return pl.BlockSpec((None, None, N_HEADS, PAGE_SIZE, HEAD_DIM),
                            lambda s, pt: (layer, pt[s, p], 0, 0, 0))

    past = n_pages * PAGE_SIZE
    grid_spec = pltpu.PrefetchScalarGridSpec(
        num_scalar_prefetch=1,
        grid=(n_seq,),
        in_specs=[tok_spec, tok_spec, tok_spec] + [page_spec(p) for p in range(n_pages)] * 2,
        out_specs=tok_spec,
        scratch_shapes=[pltpu.VMEM((past, LANE), F32), pltpu.VMEM((past, ATTN_W), BF16),
                        pltpu.VMEM((past // MOBA_BLOCK, ATTN_W), F32)],
    )
    return pl.pallas_call(
        functools.partial(_moba_sample_kernel, n_pages=n_pages, n_new=n_new),
        grid_spec=grid_spec,
        out_shape=jax.ShapeDtypeStruct((n_seq, SUBLANE, ATTN_W), F32),
        compiler_params=_params(),
        name="moba_sample",
    )(page_table, q8, kn8, vn8, *([cache_k] * n_pages), *([cache_v] * n_pages))


def _pool_prompt_kernel(halo_ref, u_ref, d_ref, *, tile, halo):
    t = pl.program_id(1)
    u = u_ref[...]
    prev = jnp.where(t > 0, halo_ref[...], 0.0)
    ucat = jnp.concatenate([prev, u], axis=0)
    hi = ucat.astype(BF16)
    lo = (ucat - hi.astype(F32)).astype(BF16)
    r = lax.broadcasted_iota(jnp.int32, (tile, halo + tile), 0) + halo
    c = lax.broadcasted_iota(jnp.int32, (tile, halo + tile), 1)
    pos = t * tile + lax.broadcasted_iota(jnp.int32, (tile, 1), 0)
    for g, w in enumerate(POOL_WINDOWS):
        band = ((c <= r) & (c > r - w)).astype(BF16)
        sl = slice(g * POOL_GW, (g + 1) * POOL_GW)
        acc = (jnp.dot(band, hi[:, sl], preferred_element_type=F32)
               + jnp.dot(band, lo[:, sl], preferred_element_type=F32))
        cnt = jnp.minimum(pos + 1, w).astype(F32)
        d_ref[:, sl] = (acc / cnt - u[:, sl]).astype(d_ref.dtype)


def _pool_prompt(proj, n_batch, seq, u_col, *, tile=256, halo=128):
    assert halo >= POOL_HIST and tile % halo == 0 and seq % tile == 0
    tiles = seq // tile
    per = tile // halo
    return pl.pallas_call(
        functools.partial(_pool_prompt_kernel, tile=tile, halo=halo),
        grid=(n_batch, tiles),
        in_specs=[pl.BlockSpec((halo, POOL_W), lambda b, t: (jnp.maximum((b * tiles + t) * per - 1, 0), u_col)),
                  pl.BlockSpec((tile, POOL_W), lambda b, t: (b * tiles + t, u_col))],
        out_specs=pl.BlockSpec((tile, POOL_W), lambda b, t: (b * tiles + t, 0)),
        out_shape=jax.ShapeDtypeStruct((n_batch * seq, POOL_W), BF16),
        compiler_params=_params(),
        name="pool_prompt",
    )(proj, proj)


def _pool_sample_kernel(h_ref, u_ref, d_ref, *, n_hist, n_new, pos0):
    for g, w in enumerate(POOL_WINDOWS):
        sl = slice(g * POOL_GW, (g + 1) * POOL_GW)
        rows = [h_ref[j, :, sl] for j in range(n_hist)] + [u_ref[t, :, sl] for t in range(n_new)]
        for t in range(n_new):
            ti = n_hist + t
            first = max(ti + 1 - w, 0)
            acc = rows[first]
            for j in range(first + 1, ti + 1):
                acc = acc + rows[j]
            cnt = float(min(pos0 + t + 1, w))
            d_ref[t, :, sl] = (acc / cnt - rows[ti]).astype(d_ref.dtype)


def _pool_sample(hist_t, u_t, pos0, *, chunk=32):
    n_hist, n_seq, _ = hist_t.shape
    n_new = u_t.shape[0]
    return pl.pallas_call(
        functools.partial(_pool_sample_kernel, n_hist=n_hist, n_new=n_new, pos0=pos0),
        grid=(n_seq // chunk,),
        in_specs=[pl.BlockSpec((n_hist, chunk, POOL_W), lambda i: (0, i, 0)),
                  pl.BlockSpec((n_new, chunk, POOL_W), lambda i: (0, i, 0))],
        out_specs=pl.BlockSpec((n_new, chunk, POOL_W), lambda i: (0, i, 0)),
        out_shape=jax.ShapeDtypeStruct((n_new, n_seq, POOL_W), BF16),
        compiler_params=_params(),
        name="pool_sample",
    )(hist_t, u_t)


def _merge_kernel(ap_ref, as_ref, dp_ref, ds_ref, ga_ref, gb_ref, wao_ref, wgrp_ref, wpo_ref, sc_ref, m_ref, *,
                  prompt_tiles):
    is_prompt = pl.program_id(0) < prompt_tiles
    attn = jnp.where(is_prompt, ap_ref[...], as_ref[...])
    dd = jnp.where(is_prompt, dp_ref[...], ds_ref[...])
    a = jnp.dot(attn, wao_ref[...], preferred_element_type=F32)
    ys = [jnp.dot(dd[:, g * POOL_GW:(g + 1) * POOL_GW], wgrp_ref[g], preferred_element_type=F32)
          for g in range(len(POOL_WINDOWS))]
    pooled = (jnp.concatenate(ys, axis=1) * sc_ref[...]).astype(BF16)
    b = jnp.dot(pooled, wpo_ref[...], preferred_element_type=F32)
    m = jax.nn.sigmoid(ga_ref[...]) * a + jax.nn.sigmoid(gb_ref[...]) * b
    m_ref[...] = m.astype(m_ref.dtype)


def _merge(attn_p, attn_s, d_p, d_s, proj, ga_col, w_ao, w_grp, w_po, scale, layer, *, tm=256):
    n = proj.shape[0]
    dm = w_ao.shape[2]
    assert attn_p.shape[0] % tm == 0 and attn_s.shape[0] % tm == 0
    pt = attn_p.shape[0] // tm
    return pl.pallas_call(
        functools.partial(_merge_kernel, prompt_tiles=pt),
        grid=(n // tm,),
        in_specs=[*_split_rows(tm, ATTN_W, pt), *_split_rows(tm, POOL_W, pt),
                  pl.BlockSpec((tm, dm), lambda i: (i, ga_col)),
                  pl.BlockSpec((tm, dm), lambda i: (i, ga_col + 1)),
                  _layer_spec(w_ao.shape[1:], layer), _layer_spec(w_grp.shape[1:], layer),
                  _layer_spec(w_po.shape[1:], layer), _layer_spec((1, POOL_W), layer)],
        out_specs=pl.BlockSpec((tm, dm), lambda i: (i, 0)),
        out_shape=jax.ShapeDtypeStruct((n, dm), BF16),
        compiler_params=_params(),
        name="merge",
    )(attn_p, attn_s, d_p, d_s, proj, proj, w_ao, w_grp, w_po, scale)


def _first_lane_of_max(vals, lane):
    top = jnp.max(vals, axis=-1, keepdims=True)
    idx = jnp.min(jnp.where(vals == top, lane, LANE), axis=-1, keepdims=True)
    return top, idx


def _one_row(ref, row):
    return ref.at[pl.ds(row, 1)]


def _outproj_router_kernel(x_ref, m_ref, wo_ref, gf_ref, wr_ref, br_ref, x1_ref, h_ref, route_ref, cnt_ref,
                           run_ref, wr3_ref):
    step = pl.program_id(0)

    d = x_ref.shape[1]

    @pl.when(step == 0)
    def _():
        run_ref[...] = jnp.zeros_like(run_ref)
        w = wr_ref[...]
        w_hi = w.astype(BF16)
        wr3_ref[:, 0:LANE] = w_hi
        wr3_ref[:, LANE:2 * LANE] = (w - w_hi.astype(F32)).astype(BF16)

    x1 = x_ref[...] + jnp.dot(m_ref[...], wo_ref[...], preferred_element_type=F32)
    x1_ref[...] = x1
    h = _rmsnorm(x1, gf_ref[...])
    h_ref[...] = h
    h_hi = h.astype(BF16)
    h_lo = (h - h_hi.astype(F32)).astype(BF16)
    hi_both = jnp.dot(h_hi, wr3_ref[...], preferred_element_type=F32)
    logits = (hi_both[:, :LANE] + hi_both[:, LANE:]
              + jnp.dot(h_lo, wr3_ref[:, 0:LANE], preferred_element_type=F32) + br_ref[...])
    lane = lax.broadcasted_iota(jnp.int32, logits.shape, 1)
    is_g = (lane >= N_EXPERTS) & (lane < N_EXPERTS + N_EXPERT_GROUPS)
    gmax, gidx = _first_lane_of_max(jnp.where(is_g, logits, -jnp.inf), lane)
    p_g = 1.0 / jnp.sum(jnp.where(is_g, jnp.exp(logits - gmax), 0.0), axis=-1, keepdims=True)
    in_grp = (lane // EXPERTS_PER_GROUP == gidx - N_EXPERTS) & (lane < N_EXPERTS)
    e1 = jnp.where(in_grp, logits, -jnp.inf)
    t1, i1 = _first_lane_of_max(e1, lane)
    t2, i2 = _first_lane_of_max(jnp.where(lane == i1, -jnp.inf, e1), lane)
    z = jnp.exp(t2 - t1)
    w1 = p_g / (1.0 + z)
    w2 = p_g * (z / (1.0 + z))
    oh1 = lane == i1
    oh2 = lane == i2
    picks = (oh1 | oh2).astype(F32)
    tm = picks.shape[0]
    earlier = (lax.broadcasted_iota(jnp.int32, (tm, tm), 1)
               < lax.broadcasted_iota(jnp.int32, (tm, tm), 0)).astype(BF16)
    before = jnp.dot(earlier, picks.astype(BF16), preferred_element_type=F32) + run_ref[...]
    r1 = jnp.sum(jnp.where(oh1, before, 0.0), axis=-1, keepdims=True)
    r2 = jnp.sum(jnp.where(oh2, before, 0.0), axis=-1, keepdims=True)
    run_ref[...] = run_ref[...] + jnp.sum(picks, axis=0, keepdims=True)
    cnt_ref[...] = jnp.broadcast_to(run_ref[...], cnt_ref.shape)
    route = jnp.zeros(logits.shape, F32)
    for k, val in enumerate((i1.astype(F32), i2.astype(F32), w1, w2, r1, r2)):
        route = jnp.where(lane == k, val, route)
    route_ref[...] = route


def _outproj_router(x, m, w_o, g_f, w_router, b_router, layer, *, tm=512):
    n, d = x.shape
    row = lambda width: pl.BlockSpec((tm, width), lambda i: (i, 0))
    return pl.pallas_call(
        _outproj_router_kernel,
        grid=(n // tm,),
        in_specs=[row(d), row(d), _layer_spec((d, d), layer), _layer_spec((1, d), layer),
                  _layer_spec((d, LANE), layer), _layer_spec((1, LANE), layer)],
        out_specs=[row(d), row(d), row(LANE), pl.BlockSpec((SUBLANE, LANE), lambda i: (0, 0))],
        out_shape=[jax.ShapeDtypeStruct((n, d), F32), jax.ShapeDtypeStruct((n, d), F32),
                   jax.ShapeDtypeStruct((n, LANE), F32), jax.ShapeDtypeStruct((SUBLANE, LANE), F32)],
        scratch_shapes=[pltpu.VMEM((1, LANE), F32), pltpu.VMEM((d, 2 * LANE), BF16)],
        compiler_params=_params(),
        name="outproj_router",
    )(x, m, w_o, g_f, w_router, b_router)


def _dispatch_tables(route, counts_f):
    n = route.shape[0]
    a = n * TOPK_IN_GROUP
    n_blocks = a // MOE_BLOCK
    assert n_blocks * MOE_BLOCK == a
    n_items = n_blocks + N_EXPERTS - 1
    counts = counts_f[0, :N_EXPERTS].astype(jnp.int32)
    end = jnp.cumsum(counts)
    start = end - counts
    eid = route[:, :TOPK_IN_GROUP].astype(jnp.int32)
    rank = route[:, 2 * TOPK_IN_GROUP:3 * TOPK_IN_GROUP].astype(jnp.int32)
    onehot = eid[:, :, None] == jnp.arange(N_EXPERTS, dtype=jnp.int32)[None, None, :]
    dest = (jnp.sum(jnp.where(onehot, start[None, None, :], 0), axis=-1) + rank).reshape(a)

    first_blk = start // MOE_BLOCK
    items_e = jnp.where(counts > 0, (end - 1) // MOE_BLOCK - first_blk + 1, 0)
    item_end = jnp.cumsum(items_e)
    item_off = item_end - items_e
    total = item_end[-1]
    idx = jnp.arange(n_items, dtype=jnp.int32)
    e = jnp.minimum(jnp.sum((idx[:, None] >= item_end[None, :]).astype(jnp.int32), axis=1), N_EXPERTS - 1)
    pick = lambda tbl, ee: jnp.sum(jnp.where(ee[:, None] == jnp.arange(N_EXPERTS)[None, :], tbl[None, :], 0), axis=1)
    blk = pick(first_blk, e) + idx - pick(item_off, e)
    lo = jnp.maximum(pick(start, e), blk * MOE_BLOCK) - blk * MOE_BLOCK
    hi = jnp.minimum(pick(end, e), (blk + 1) * MOE_BLOCK) - blk * MOE_BLOCK
    valid = idx < total
    e_last = jnp.sum(jnp.where(idx == total - 1, e, 0))
    e = jnp.where(valid, e, e_last)
    blk = jnp.where(valid, blk, n_blocks - 1)
    lo = jnp.where(valid, lo, 0)
    hi = jnp.where(valid, hi, 0)
    first = jnp.concatenate([jnp.ones((1,), jnp.int32), (blk[1:] != blk[:-1]).astype(jnp.int32)])
    ids = jnp.arange(N_EXPERTS, dtype=jnp.int32)
    later = (ids[None, :] > ids[:, None]) & (counts[None, :] > 0)
    nxt_e = jnp.min(jnp.where(later, ids[None, :], N_EXPERTS), axis=1)
    nxt = pick(jnp.where(nxt_e < N_EXPERTS, nxt_e, -1), e)
    i32 = lambda t: t.astype(jnp.int32)
    return i32(dest), i32(blk), i32(e), i32(lo), i32(hi), first, i32(nxt)


def _dispatch_kernel(dest_ref, h_ref, xs_hbm, sem, *, tm):
    base = pl.program_id(0) * tm * TOPK_IN_GROUP

    def start(g, carry):
        r0 = pl.multiple_of(g * SUBLANE, SUBLANE)
        for u in range(SUBLANE):
            for k in range(TOPK_IN_GROUP):
                slot = dest_ref[base + (r0 + u) * TOPK_IN_GROUP + k]
                pltpu.make_async_copy(_one_row(h_ref, r0 + u), _one_row(xs_hbm, slot), sem).start(priority=k)
        return carry
    lax.fori_loop(0, tm // SUBLANE, start, 0)

    def wait(r, carry):
        for k in range(TOPK_IN_GROUP):
            pltpu.make_async_copy(_one_row(h_ref, 0), _one_row(xs_hbm, 0), sem).wait()
        return carry
    lax.fori_loop(0, tm, wait, 0, unroll=8)


def _dispatch(dest, h, *, tm=512):
    n, d = h.shape
    grid_spec = pltpu.PrefetchScalarGridSpec(
        num_scalar_prefetch=1,
        grid=(n // tm,),
        in_specs=[pl.BlockSpec((tm, d), lambda i, dr: (i, 0))],
        out_specs=pl.BlockSpec(memory_space=pl.ANY),
        scratch_shapes=[pltpu.SemaphoreType.DMA(())],
    )
    return pl.pallas_call(
        functools.partial(_dispatch_kernel, tm=tm),
        grid_spec=grid_spec,
        out_shape=jax.ShapeDtypeStruct((n * TOPK_IN_GROUP, d), F32),
        compiler_params=_params(),
        name="moe_dispatch",
    )(dest, h)


def _moe_ffn_kernel(blk_ref, e_ref, lo_ref, hi_ref, first_ref, nxt_ref, x_ref, wg_hbm, wu_hbm, wd_hbm, y_ref,
                    stage_g, stage_u, stage_d, wg_b, wu_b, wd_b, sems, cur_ref, *, layer):
    del blk_ref
    j = pl.program_id(0)

    def weight_copies(expert, s):
        return (pltpu.make_async_copy(wg_hbm.at[layer, expert], stage_g.at[s], sems.at[s]),
                pltpu.make_async_copy(wu_hbm.at[layer, expert], stage_u.at[s], sems.at[s]),
                pltpu.make_async_copy(wd_hbm.at[layer, expert], stage_d.at[s], sems.at[s]))

    @pl.when(j == 0)
    def _():
        cur_ref[0] = 1
        for c in weight_copies(e_ref[0], 0):
            c.start(priority=1)

    @pl.when((j == 0) | (e_ref[j] != e_ref[jnp.maximum(j - 1, 0)]))
    def _():
        s = 1 - cur_ref[0]
        cur_ref[0] = s

        @pl.when(nxt_ref[j] >= 0)
        def _():
            for c in weight_copies(nxt_ref[j], 1 - s):
                c.start(priority=1)

        for c in weight_copies(e_ref[j], s):
            c.wait()
        wg_b[...] = stage_g[s].astype(BF16)
        wu_b[...] = stage_u[s].astype(BF16)
        wd_b[...] = stage_d[s].astype(BF16)

    @pl.when(hi_ref[j] > lo_ref[j])
    def _():
        x = x_ref[...].astype(BF16)
        gate = jnp.dot(x, wg_b[...], preferred_element_type=F32)
        up = jnp.dot(x, wu_b[...], preferred_element_type=F32)
        act = (jax.nn.silu(gate) * up).astype(BF16)
        y = jnp.dot(act, wd_b[...], preferred_element_type=F32)
        row = lax.broadcasted_iota(jnp.int32, (MOE_BLOCK, 1), 0)
        y = jnp.where((row >= lo_ref[j]) & (row < hi_ref[j]), y, 0.0)

        @pl.when(first_ref[j] == 1)
        def _():
            y_ref[...] = y

        @pl.when(first_ref[j] == 0)
        def _():
            y_ref[...] = y_ref[...] + y


def _moe_ffn(xs, blk, e, lo, hi, first, nxt, w_gate, w_up, w_down, layer):
    a, d = xs.shape
    de = w_gate.shape[3]
    n_items = blk.shape[0]
    slab = pl.BlockSpec((MOE_BLOCK, d), lambda j, blk, *_: (blk[j], 0))
    hbm = pl.BlockSpec(memory_space=pl.ANY)
    grid_spec = pltpu.PrefetchScalarGridSpec(
        num_scalar_prefetch=6,
        grid=(n_items,),
        in_specs=[slab, hbm, hbm, hbm],
        out_specs=slab,
        scratch_shapes=[pltpu.VMEM((2, d, de), F32), pltpu.VMEM((2, d, de), F32), pltpu.VMEM((2, de, d), F32),
                        pltpu.VMEM((d, de), BF16), pltpu.VMEM((d, de), BF16), pltpu.VMEM((de, d), BF16),
                        pltpu.SemaphoreType.DMA((2,)), pltpu.SMEM((1,), jnp.int32)],
    )
    return pl.pallas_call(
        functools.partial(_moe_ffn_kernel, layer=layer),
        grid_spec=grid_spec,
        out_shape=jax.ShapeDtypeStruct((a, d), F32),
        compiler_params=_params(),
        name="moe_ffn",
    )(blk, e, lo, hi, first, nxt, xs, w_gate, w_up, w_down)


def _combine_ple_kernel(dest_ref, x_ref, route_ref, yb_hbm, pp_ref, ps_ref, gp_ref, wplg_ref, wple_ref, gfin_ref,
                        *rest, tm, prompt_tiles, final):
    outs, (ybuf, sems) = rest[:-2], rest[-2:]
    i = pl.program_id(0)
    nt = pl.num_programs(0)
    slot = i % 2

    def start_gather(tile, s):
        def body(g, carry):
            r0 = pl.multiple_of(g * SUBLANE, SUBLANE)
            for u in range(SUBLANE):
                for k in range(TOPK_IN_GROUP):
                    src = dest_ref[(tile * tm + r0 + u) * TOPK_IN_GROUP + k]
                    pltpu.make_async_copy(_one_row(yb_hbm, src), _one_row(ybuf.at[s, k], r0 + u),
                                          sems.at[s]).start(priority=k)
            return carry
        lax.fori_loop(0, tm // SUBLANE, body, 0)

    @pl.when(i == 0)
    def _():
        start_gather(0, 0)

    @pl.when(i + 1 < nt)
    def _():
        start_gather(i + 1, 1 - slot)

    def wait_body(r, carry):
        for k in range(TOPK_IN_GROUP):
            pltpu.make_async_copy(_one_row(yb_hbm, 0), _one_row(ybuf.at[slot, k], 0), sems.at[slot]).wait()
        return carry
    lax.fori_loop(0, tm, wait_body, 0, unroll=8)

    route = route_ref[...]
    moe = sum(ybuf[slot, k] * route[:, TOPK_IN_GROUP + k:TOPK_IN_GROUP + k + 1] for k in range(TOPK_IN_GROUP))
    x2 = x_ref[...] + moe
    hp = _rmsnorm(x2, gp_ref[...]).astype(BF16)
    gate = jax.nn.sigmoid(jnp.dot(hp, wplg_ref[...], preferred_element_type=F32))
    p = jnp.where(i < prompt_tiles, pp_ref[...], ps_ref[...])
    pe = jnp.dot(p.astype(BF16), wple_ref[...], preferred_element_type=F32)
    x3 = x2 + gate * pe
    if not final:
        outs[0][...] = x3
    else:
        y = _rmsnorm(x3, gfin_ref[...])

        @pl.when(i < prompt_tiles)
        def _():
            outs[0][...] = y

        @pl.when(i >= prompt_tiles)
        def _():
            outs[1][...] = y


def _split_rows(tm, width, pt, lead=()):
    none = (None,) * len(lead)
    return (pl.BlockSpec(none + (tm, width), lambda i, *_: lead + (jnp.minimum(i, pt - 1), 0)),
            pl.BlockSpec(none + (tm, width), lambda i, *_: lead + (jnp.maximum(i - pt, 0), 0)))


def _combine_ple(dest, x1, route, yb, p_prompt, p_sample, g_p, w_plg, w_ple, g_final, layer, *, final, tm=256):
    n, d = x1.shape
    n_prompt, pd = p_prompt.shape[1:]
    assert n_prompt % tm == 0 and (n - n_prompt) % tm == 0
    pt = n_prompt // tm
    row = lambda width: pl.BlockSpec((tm, width), lambda i, dr: (i, 0))
    if final:
        out_specs = list(_split_rows(tm, d, pt))
        out_shape = [jax.ShapeDtypeStruct((n_prompt, d), F32), jax.ShapeDtypeStruct((n - n_prompt, d), F32)]
    else:
        out_specs = [row(d)]
        out_shape = [jax.ShapeDtypeStruct((n, d), F32)]
    grid_spec = pltpu.PrefetchScalarGridSpec(
        num_scalar_prefetch=1,
        grid=(n // tm,),
        in_specs=[row(d), row(LANE), pl.BlockSpec(memory_space=pl.ANY), *_split_rows(tm, pd, pt, (layer,)),
                  _layer_spec((1, d), layer), _layer_spec((d, d), layer), _layer_spec((pd, d), layer),
                  pl.BlockSpec((1, d), lambda i, dr: (0, 0))],
        out_specs=out_specs,
        scratch_shapes=[pltpu.VMEM((2, TOPK_IN_GROUP, tm, d), F32), pltpu.SemaphoreType.DMA((2,))],
    )
    return pl.pallas_call(
        functools.partial(_combine_ple_kernel, tm=tm, prompt_tiles=pt, final=final),
        grid_spec=grid_spec,
        out_shape=out_shape,
        compiler_params=_params(),
        name="combine_ple",
    )(dest, x1, route, yb, p_prompt, p_sample, g_p, w_plg, w_ple, g_final.reshape(1, d))


def kernel(x_prompt, x_sample, cache_k, cache_v, state_pool, page_table, p_prompt, p_sample, w_in, w_attn_out, w_pool_out, w_out, w_pool_group, pool_scale, g_mix, g_ffn, g_ple, g_final, w_router_group, b_router_group, w_router_expert, b_router_expert, w_exp_gate, w_exp_up, w_exp_down, w_ple, w_ple_gate):
    nb_, seq, d = x_prompt.shape
    n_seq, n_new, _ = x_sample.shape
    depth = w_in.shape[0]
    n_p = nb_ * seq
    n_s = n_seq * n_new
    past_len = page_table.shape[1] * PAGE_SIZE
    u_col = 3 * ATTN_W // POOL_W
    ga_col = (3 * ATTN_W + POOL_W) // d
    assert 3 * ATTN_W == u_col * POOL_W and 3 * ATTN_W + POOL_W == ga_col * d

    x = jnp.concatenate([x_prompt.reshape(n_p, d), x_sample.reshape(n_s, d)], axis=0)
    outs = {name: [] for name in ("pool_p", "ks", "vs", "pool_s")}
    pad_new = lambda t: jnp.pad(t.reshape(n_seq, n_new, ATTN_W), ((0, 0), (0, SUBLANE - n_new), (0, 0)))
    heads_s = lambda t: t.reshape(n_seq, n_new, N_HEADS, HEAD_DIM).transpose(0, 2, 1, 3)

    per_layer_row = lambda g: g.reshape(depth, 1, g.shape[-1])
    w_in_b, w_ao_b, w_po_b, w_o_b, w_grp_b, w_plg_b, w_ple_b = (
        w.astype(BF16) for w in (w_in, w_attn_out, w_pool_out, w_out, w_pool_group, w_ple_gate, w_ple))
    lane_pad = LANE - N_EXPERTS - N_EXPERT_GROUPS
    w_router = jnp.pad(jnp.concatenate([w_router_expert, w_router_group], axis=2), ((0, 0), (0, 0), (0, lane_pad)))
    b_router = jnp.pad(jnp.concatenate([b_router_expert, b_router_group], axis=1),
                       ((0, 0), (0, lane_pad))).reshape(depth, 1, LANE)
    pp, ps = p_prompt.reshape(depth, n_p, -1), p_sample.reshape(depth, n_s, -1)
    kv_p = None

    for i in range(depth):
        proj = _inproj(x, per_layer_row(g_mix), w_in_b, i)
        attn_p, k_all, v_all = _moba_prompt(proj, nb_, seq, i, depth, kv_p)
        kv_p = (k_all, v_all)
        q_s, k_s, v_s = (proj[n_p:, c * ATTN_W:(c + 1) * ATTN_W] for c in range(3))
        attn_s = _moba_sample(pad_new(q_s), pad_new(k_s), pad_new(v_s), cache_k, cache_v, page_table, i, n_new)
        attn_s = attn_s[:, :n_new].reshape(n_s, ATTN_W).astype(BF16)
        u_tail = jnp.stack([lax.slice(proj, ((b + 1) * seq - POOL_HIST, 3 * ATTN_W), ((b + 1) * seq, 3 * ATTN_W + POOL_W))
                            for b in range(nb_)])
        u_s = proj[n_p:, 3 * ATTN_W:3 * ATTN_W + POOL_W].reshape(n_seq, n_new, POOL_W)
        d_p = _pool_prompt(proj, nb_, seq, u_col)
        d_s = _pool_sample(state_pool[i].transpose(1, 0, 2), u_s.transpose(1, 0, 2), past_len)
        d_s = d_s.transpose(1, 0, 2).reshape(n_s, POOL_W)
        m = _merge(attn_p, attn_s, d_p, d_s, proj, ga_col, w_ao_b, w_grp_b, w_po_b, per_layer_row(pool_scale), i)
        x1, h, route, counts = _outproj_router(x, m, w_o_b, per_layer_row(g_ffn), w_router, b_router, i)
        dest, blk, e, lo, hi, first, nxt = _dispatch_tables(route, counts)
        xs = _dispatch(dest, h)
        yb = _moe_ffn(xs, blk, e, lo, hi, first, nxt, w_exp_gate, w_exp_up, w_exp_down, i)
        res = _combine_ple(dest, x1, route, yb, pp, ps, per_layer_row(g_ple), w_plg_b, w_ple_b, g_final, i,
                           final=(i == depth - 1))
        x = res[0]

        outs["pool_p"].append(u_tail)
        outs["ks"].append(heads_s(k_s))
        outs["vs"].append(heads_s(v_s))
        outs["pool_s"].append(jnp.concatenate([state_pool[i], u_s], axis=1)[:, n_new:])

    y_prompt = res[0].reshape(nb_, seq, d)
    y_sample = res[1].reshape(n_seq, n_new, d)
    stack = lambda name: jnp.stack(outs[name])
    return (y_prompt, y_sample, kv_p[0], kv_p[1], stack("pool_p"),
            stack("ks"), stack("vs"), stack("pool_s"))
```
